```python
import math
import jax, jax.numpy as jnp
from jax import lax
import numpy as np

D_MODEL = 1024
BATCH = 4
SEQ = 8192
DEPTH = 2

GRID_W = 64
Q_BLOCK = 128
HEAD_DIM = 64
EPS = 1e-6
A_HEADS = 8
A_KV_HEADS = 2
A_GROUP = A_HEADS // A_KV_HEADS
A_WIDTH = A_HEADS * HEAD_DIM
A_KV_WIDTH = A_KV_HEADS * HEAD_DIM
ROPE_THETA = 10000.0
B_WIDTH = D_MODEL // 2
B_GROUPS = 8
CONV_WIDTH = 31
EVEN_IN = A_WIDTH + 2 * A_KV_WIDTH + 2 * B_WIDTH
EVEN_MIX = A_WIDTH + B_WIDTH
C_HEADS = 8
C_HEAD_DIM = D_MODEL // (2 * C_HEADS)
C_WIDTH = 2 * C_HEADS * C_HEAD_DIM
ODD_IN = 3 * C_WIDTH
REL_BUCKETS = 32
REL_MAX_DIST = 128
MOE_GROUPS = 4
MOE_PER_GROUP = 8
MOE_EXPERTS = MOE_GROUPS * MOE_PER_GROUP
MOE_TOPK = 2
MOE_HIDDEN = 512
MOE_BLOCK = 128
N_EVEN = (DEPTH + 1) // 2
N_ODD = DEPTH // 2

kernel_name = 'hybrid_gqa_conformer_diffattn_hmoe_encoder'


def _rmsnorm(x, g, eps=EPS):
    xf = x.astype(jnp.float32)
    y = xf * lax.rsqrt(jnp.mean(xf * xf, axis=-1, keepdims=True) + eps)
    return (y * g.astype(jnp.float32)).astype(x.dtype)


def _rope_1d(x, pos):
    m = x.shape[-1] // 2
    inv = ROPE_THETA ** (-jnp.arange(m, dtype=jnp.float32) / m)
    ang = pos.astype(jnp.float32)[:, None] * inv[None, :]
    cos = jnp.cos(ang)[:, None, :]
    sin = jnp.sin(ang)[:, None, :]
    xf = x.astype(jnp.float32)
    x1, x2 = xf[..., :m], xf[..., m:]
    return jnp.concatenate([x1 * cos - x2 * sin, x2 * cos + x1 * sin], axis=-1).astype(x.dtype)


def _grid_positions(seq_len):
    rows = seq_len // GRID_W
    r, c = jnp.meshgrid(jnp.arange(rows, dtype=jnp.int32), jnp.arange(GRID_W, dtype=jnp.int32), indexing='ij')
    return r.reshape(-1), c.reshape(-1)


def _axial_rope(x, row, col):
    half = x.shape[-1] // 2
    return jnp.concatenate([_rope_1d(x[..., :half], row), _rope_1d(x[..., half:], col)], axis=-1)


def _t5_bucket(rel):
    nb = REL_BUCKETS // 2
    max_exact = nb // 2
    ret = jnp.where(rel > 0, nb, 0).astype(jnp.int32)
    n = jnp.abs(rel)
    nf = jnp.maximum(n, 1).astype(jnp.float32)
    large = max_exact + (jnp.log(nf / max_exact) / math.log(REL_MAX_DIST / max_exact) * (nb - max_exact)).astype(jnp.int32)
    large = jnp.minimum(large, nb - 1)
    return ret + jnp.where(n < max_exact, n, large)


def _gqa_axial(q, k, v, q_norm, k_norm):
    B, S = q.shape[0], q.shape[1]
    row, col = _grid_positions(S)
    q = _axial_rope(_rmsnorm(q, q_norm), row, col)
    k = _axial_rope(_rmsnorm(k, k_norm), row, col)
    nb = S // Q_BLOCK
    qb = q.reshape(B, nb, Q_BLOCK, A_KV_HEADS, A_GROUP, HEAD_DIM).transpose(1, 0, 3, 4, 2, 5)
    kt = k.transpose(0, 2, 1, 3)
    vt = v.transpose(0, 2, 1, 3)
    scale = HEAD_DIM ** -0.5

    def block(qblk):
        s = jnp.einsum('bkgqd,bksd->bkgqs', qblk, kt, preferred_element_type=jnp.float32) * scale
        p = jax.nn.softmax(s, axis=-1)
        return jnp.einsum('bkgqs,bksd->bkgqd', p.astype(vt.dtype), vt)

    o = lax.map(block, qb)
    return o.transpose(1, 0, 4, 2, 3, 5).reshape(B, S, A_WIDTH)


def _conformer_conv(u, conv_w, conv_b, gn_g, gn_b):
    B, S = u.shape[0], u.shape[1]
    a, gate = jnp.split(u, 2, axis=-1)
    h = a * jax.nn.sigmoid(gate)
    h = lax.conv_general_dilated(h, conv_w[:, None, :].astype(h.dtype), window_strides=(1,),
                                 padding=[(CONV_WIDTH // 2, CONV_WIDTH // 2)],
                                 dimension_numbers=('NWC', 'WIO', 'NWC'),
                                 feature_group_count=B_WIDTH) + conv_b.astype(h.dtype)
    hf = h.reshape(B, S, B_GROUPS, B_WIDTH // B_GROUPS).astype(jnp.float32)
    mu = jnp.mean(hf, axis=-1, keepdims=True)
    var = jnp.mean(jnp.square(hf - mu), axis=-1, keepdims=True)
    hn = ((hf - mu) * lax.rsqrt(var + 1e-5)).reshape(B, S, B_WIDTH)
    hn = hn * gn_g.astype(jnp.float32) + gn_b.astype(jnp.float32)
    return jax.nn.silu(hn).astype(u.dtype)


def _even_mixer(h, w_in, q_norm, k_norm, conv_w, conv_b, gn_g, gn_b, w_out):
    B, S, _ = h.shape
    z = h @ w_in
    q, k, v, u = jnp.split(z, [A_WIDTH, A_WIDTH + A_KV_WIDTH, A_WIDTH + 2 * A_KV_WIDTH], axis=-1)
    ya = _gqa_axial(q.reshape(B, S, A_HEADS, HEAD_DIM), k.reshape(B, S, A_KV_HEADS, HEAD_DIM),
                    v.reshape(B, S, A_KV_HEADS, HEAD_DIM), q_norm, k_norm)
    yb = _conformer_conv(u, conv_w, conv_b, gn_g, gn_b)
    return jnp.concatenate([ya, yb], axis=-1) @ w_out


def _diff_attention(h, w_in, lq1, lk1, lq2, lk2, subln, w_out, rel_bias, layer_idx):
    B, S, _ = h.shape
    z = h @ w_in
    q, k, v = jnp.split(z, 3, axis=-1)
    q = q.reshape(B, S, C_HEADS, 2, C_HEAD_DIM)
    k = k.reshape(B, S, C_HEADS, 2, C_HEAD_DIM)
    v = v.reshape(B, S, C_HEADS, 2 * C_HEAD_DIM)
    lam_init = 0.8 - 0.6 * math.exp(-0.3 * layer_idx)
    lam = (jnp.exp(jnp.sum(lq1.astype(jnp.float32) * lk1.astype(jnp.float32)))
           - jnp.exp(jnp.sum(lq2.astype(jnp.float32) * lk2.astype(jnp.float32))) + lam_init)
    nb = S // Q_BLOCK
    qb = q.reshape(B, nb, Q_BLOCK, C_HEADS, 2, C_HEAD_DIM).transpose(1, 0, 3, 4, 2, 5)
    kt = k.transpose(0, 2, 3, 1, 4)
    vt = v.transpose(0, 2, 1, 3)
    kpos = jnp.arange(S, dtype=jnp.int32)
    scale = C_HEAD_DIM ** -0.5

    def block(args):
        qblk, i = args
        qpos = i * Q_BLOCK + jnp.arange(Q_BLOCK, dtype=jnp.int32)
        bias = rel_bias[_t5_bucket(kpos[None, :] - qpos[:, None])]
        bias = bias.astype(jnp.float32).transpose(2, 0, 1)
        s = jnp.einsum('bhcqd,bhcsd->bhcqs', qblk, kt, preferred_element_type=jnp.float32) * scale
        p = jax.nn.softmax(s + bias[None, :, None], axis=-1)
        a = p[:, :, 0] - lam * p[:, :, 1]
        return jnp.einsum('bhqs,bhsd->bhqd', a.astype(vt.dtype), vt)

    o = lax.map(block, (qb, jnp.arange(nb, dtype=jnp.int32)))
    o = o.transpose(1, 0, 3, 2, 4).reshape(B, S, C_HEADS, 2 * C_HEAD_DIM)
    o = _rmsnorm(o, subln, 1e-5) * (1.0 - lam_init)
    return o.reshape(B, S, C_WIDTH) @ w_out


def _hier_moe(h, w_group, b_group, w_expert, b_expert, w1, w3, w2):
    B, S, D = h.shape
    N = B * S
    xt = h.reshape(N, D)
    pg = jax.nn.softmax((xt @ w_group).astype(jnp.float32) + b_group.astype(jnp.float32), axis=-1)
    g_val, g_idx = lax.top_k(pg, 1)
    g_val, g_idx = g_val[:, 0], g_idx[:, 0]
    el = ((xt @ w_expert).astype(jnp.float32) + b_expert.astype(jnp.float32)).reshape(N, MOE_GROUPS, MOE_PER_GROUP)
    el_g = el[jnp.arange(N), g_idx]
    pe = jax.nn.softmax(el_g, axis=-1)
    e_val, e_idx = lax.top_k(pe, MOE_TOPK)
    gate = g_val[:, None] * e_val / jnp.sum(e_val, axis=-1, keepdims=True)
    expert_id = g_idx[:, None] * MOE_PER_GROUP + e_idx
    A = N * MOE_TOPK
    flat_e = expert_id.reshape(-1).astype(jnp.int32)
    flat_w = gate.reshape(-1)
    flat_tok = jnp.arange(A, dtype=jnp.int32) // MOE_TOPK
    order = jnp.argsort(flat_e)
    se = flat_e[order]
    counts = jnp.bincount(flat_e, length=MOE_EXPERTS).astype(jnp.int32)
    pcounts = (counts + MOE_BLOCK - 1) // MOE_BLOCK * MOE_BLOCK
    starts = jnp.cumsum(counts) - counts
    pends = jnp.cumsum(pcounts)
    pstarts = pends - pcounts
    dest = pstarts[se] + (jnp.arange(A, dtype=jnp.int32) - starts[se])
    P = A + MOE_EXPERTS * MOE_BLOCK
    n_blk = P // MOE_BLOCK
    buf_tok = jnp.full((P,), N, dtype=jnp.int32).at[dest].set(flat_tok[order])
    buf_w = jnp.zeros((P,), dtype=h.dtype).at[dest].set(flat_w[order].astype(h.dtype))
    blk_e = jnp.minimum(jnp.searchsorted(pends, jnp.arange(n_blk, dtype=jnp.int32) * MOE_BLOCK, side='right'),
                        MOE_EXPERTS - 1).astype(jnp.int32)
    xb = xt.at[buf_tok].get(mode='fill', fill_value=0).reshape(n_blk, MOE_BLOCK, D)

    def expert_block(args):
        xblk, e = args
        return (jax.nn.silu(xblk @ w1[e]) * (xblk @ w3[e])) @ w2[e]

    yb = lax.map(expert_block, (xb, blk_e)).reshape(P, D)
    y = jnp.zeros((N, D), dtype=h.dtype).at[buf_tok].add(yb * buf_w[:, None], mode='drop')
    return y.reshape(B, S, D)


def setup_inputs(seed: int = 0) -> dict:
    key = jax.random.key(seed)
    ks = jax.random.split(key, 32)
    f32 = jnp.float32

    def nrm(k, shape, scale):
        return jax.random.normal(k, shape, f32) * scale

    def gain(k, shape):
        return 1.0 + 0.02 * jax.random.normal(k, shape, f32)

    D = D_MODEL
    return {
        'x': jax.random.normal(ks[0], (BATCH, SEQ, D), f32),
        'norm_mix': gain(ks[1], (DEPTH, D)),
        'norm_ffn': gain(ks[2], (DEPTH, D)),
        'norm_final': gain(ks[3], (D,)),
        'ev_w_in': nrm(ks[4], (N_EVEN, D, EVEN_IN), D ** -0.5),
        'ev_q_norm': gain(ks[5], (N_EVEN, HEAD_DIM)),
        'ev_k_norm': gain(ks[6], (N_EVEN, HEAD_DIM)),
        'ev_conv_w': nrm(ks[7], (N_EVEN, CONV_WIDTH, B_WIDTH), CONV_WIDTH ** -0.5),
        'ev_conv_b': nrm(ks[8], (N_EVEN, B_WIDTH), 0.02),
        'ev_gn_g': gain(ks[9], (N_EVEN, B_WIDTH)),
        'ev_gn_b': nrm(ks[10], (N_EVEN, B_WIDTH), 0.02),
        'ev_w_out': nrm(ks[11], (N_EVEN, EVEN_MIX, D), EVEN_MIX ** -0.5),
        'od_w_in': nrm(ks[12], (N_ODD, D, ODD_IN), D ** -0.5),
        'od_lam_q1': nrm(ks[13], (N_ODD, C_HEAD_DIM), 0.1),
        'od_lam_k1': nrm(ks[14], (N_ODD, C_HEAD_DIM), 0.1),
        'od_lam_q2': nrm(ks[15], (N_ODD, C_HEAD_DIM), 0.1),
        'od_lam_k2': nrm(ks[16], (N_ODD, C_HEAD_DIM), 0.1),
        'od_subln': gain(ks[17], (N_ODD, 2 * C_HEAD_DIM)),
        'od_w_out': nrm(ks[18], (N_ODD, C_WIDTH, D), C_WIDTH ** -0.5),
        'rel_bias': nrm(ks[19], (REL_BUCKETS, C_HEADS), 0.5),
        'moe_w_group': nrm(ks[20], (DEPTH, D, MOE_GROUPS), D ** -0.5),
        'moe_b_group': nrm(ks[21], (DEPTH, MOE_GROUPS), 0.01),
        'moe_w_expert': nrm(ks[22], (DEPTH, D, MOE_EXPERTS), D ** -0.5),
        'moe_b_expert': nrm(ks[23], (DEPTH, MOE_EXPERTS), 0.01),
        'moe_w1': nrm(ks[24], (DEPTH, MOE_EXPERTS, D, MOE_HIDDEN), D ** -0.5),
        'moe_w3': nrm(ks[25], (DEPTH, MOE_EXPERTS, D, MOE_HIDDEN), D ** -0.5),
        'moe_w2': nrm(ks[26], (DEPTH, MOE_EXPERTS, MOE_HIDDEN, D), MOE_HIDDEN ** -0.5),
    }


def reference(x, norm_mix, norm_ffn, norm_final, ev_w_in, ev_q_norm, ev_k_norm, ev_conv_w, ev_conv_b,
              ev_gn_g, ev_gn_b, ev_w_out, od_w_in, od_lam_q1, od_lam_k1, od_lam_q2, od_lam_k2, od_subln,
              od_w_out, rel_bias, moe_w_group, moe_b_group, moe_w_expert, moe_b_expert, moe_w1, moe_w3, moe_w2):
    for l in range(DEPTH):
        h = _rmsnorm(x, norm_mix[l])
        if l % 2 == 0:
            j = l // 2
            x = x + _even_mixer(h, ev_w_in[j], ev_q_norm[j], ev_k_norm[j], ev_conv_w[j], ev_conv_b[j],
                                ev_gn_g[j], ev_gn_b[j], ev_w_out[j])
        else:
            j = l // 2
            x = x + _diff_attention(h, od_w_in[j], od_lam_q1[j], od_lam_k1[j], od_lam_q2[j], od_lam_k2[j],
                                    od_subln[j], od_w_out[j], rel_bias, l)
        h = _rmsnorm(x, norm_ffn[l])
        x = x + _hier_moe(h, moe_w_group[l], moe_b_group[l], moe_w_expert[l], moe_b_expert[l],
                          moe_w1[l], moe_w3[l], moe_w2[l])
    return _rmsnorm(x, norm_final)
```

```python
import functools
import math

import jax
import jax.numpy as jnp
from jax import lax
from jax.experimental import pallas as pl
from jax.experimental.pallas import tpu as pltpu

F32 = jnp.float32
BF16 = jnp.bfloat16

GRID_W = 64
HEAD_DIM = 64
EPS = 1e-6
A_HEADS = 8
A_KV_HEADS = 2
A_GROUP = A_HEADS // A_KV_HEADS
ROPE_THETA = 10000.0
ROPE_HALF = HEAD_DIM // 4
B_GROUP_WIDTH = 64
CONV_WIDTH = 31
CONV_HALO = 16
C_HEADS = 8
C_HEAD_DIM = 64
REL_BUCKETS = 32
REL_MAX_DIST = 128
MOE_GROUPS = 4
MOE_PER_GROUP = 8
MOE_EXPERTS = MOE_GROUPS * MOE_PER_GROUP
MOE_TOPK = 2
LOG2E = 1.4426950408889634
NEG_BIG = -1e30
LANES = 128

TM_PROJ = 512
TQ_A = 512
TK_A = 512
TQ_C = 256
TK_C = 256
TS_CONV = 256
CONV_SUB = 64
MOE_BM = 128
VMEM_LIMIT = 56 * 1024 * 1024


def _cparams(*sem):
    return pltpu.CompilerParams(dimension_semantics=sem, vmem_limit_bytes=VMEM_LIMIT)


def _split_dot(a, b_bf16):
    hi = a.astype(BF16)
    lo = (a - hi.astype(F32)).astype(BF16)
    return (jnp.dot(hi, b_bf16, preferred_element_type=F32)
            + jnp.dot(lo, b_bf16, preferred_element_type=F32))


def _rms_rows(x, g, eps):
    ms = jnp.mean(x * x, axis=-1, keepdims=True)
    return x * lax.rsqrt(ms + eps) * g


def _sigmoid(x):
    return 1.0 / (1.0 + jnp.exp(-x))


def _even_in_kernel(x_ref, g_ref, w_ref, gqk_ref, bd_ref, cos_ref, s1_ref, s2_ref,
                    q_ref, k_ref, v_ref, glu_ref, *, qk_w, q_w, kv_w, b_w):
    h = _rms_rows(x_ref[...], g_ref[...], EPS)
    z = jnp.dot(h.astype(BF16), w_ref[...], preferred_element_type=F32)
    qk = z[:, :qk_w]
    ms = _split_dot(qk * qk, bd_ref[...])
    qkn = qk * lax.rsqrt(ms + EPS) * gqk_ref[...]
    cos, s1, s2 = cos_ref[...], s1_ref[...], s2_ref[...]
    parts = []
    for c in range(qk_w // LANES):
        blk = qkn[:, c * LANES:(c + 1) * LANES]
        up = pltpu.roll(blk, LANES - ROPE_HALF, 1)
        dn = pltpu.roll(blk, ROPE_HALF, 1)
        parts.append(blk * cos + up * s1 + dn * s2)
    rot = jnp.concatenate(parts, axis=1)
    q_ref[...] = rot[:, :q_w].astype(BF16)
    k_ref[...] = rot[:, q_w:qk_w].astype(BF16)
    v_ref[...] = z[:, qk_w:qk_w + kv_w].astype(BF16)
    a = z[:, qk_w + kv_w:qk_w + kv_w + b_w]
    gate = z[:, qk_w + kv_w + b_w:qk_w + kv_w + 2 * b_w]
    glu_ref[...] = a * _sigmoid(gate)


def _even_in(x2d, g, w_in, q_norm, k_norm, seq):
    n, d = x2d.shape
    q_w, kv_w, b_w = A_HEADS * HEAD_DIM, A_KV_HEADS * HEAD_DIM, d // 2
    qk_w = q_w + kv_w
    tm = TM_PROJ
    gq = jnp.tile(q_norm.astype(F32), A_HEADS) * (HEAD_DIM ** -0.5 * LOG2E)
    gk = jnp.tile(k_norm.astype(F32), A_KV_HEADS)
    gqk = jnp.concatenate([gq, gk])[None, :]
    hid = jnp.arange(qk_w) // HEAD_DIM
    bd = (hid[:, None] == hid[None, :]).astype(BF16) * (1.0 / HEAD_DIM)
    pos = jnp.arange(seq, dtype=jnp.int32)
    row, col = pos // GRID_W, pos % GRID_W
    m = ROPE_HALF
    inv = ROPE_THETA ** (-jnp.arange(m, dtype=F32) / m)
    ang_r = row.astype(F32)[:, None] * inv[None, :]
    ang_c = col.astype(F32)[:, None] * inv[None, :]
    zero = jnp.zeros_like(ang_r)
    cos64 = jnp.concatenate([jnp.cos(ang_r), jnp.cos(ang_r), jnp.cos(ang_c), jnp.cos(ang_c)], axis=1)
    s1_64 = jnp.concatenate([-jnp.sin(ang_r), zero, -jnp.sin(ang_c), zero], axis=1)
    s2_64 = jnp.concatenate([zero, jnp.sin(ang_r), zero, jnp.sin(ang_c)], axis=1)
    rep = LANES // HEAD_DIM
    cos_t, s1_t, s2_t = (jnp.tile(t, (1, rep)) for t in (cos64, s1_64, s2_64))
    nseq = seq // tm
    tab_spec = pl.BlockSpec((tm, LANES), lambda i: (i % nseq, 0))
    full = lambda shape: pl.BlockSpec(shape, lambda i: (0,) * len(shape))
    kern = functools.partial(_even_in_kernel, qk_w=qk_w, q_w=q_w, kv_w=kv_w, b_w=b_w)
    return pl.pallas_call(
        kern,
        grid=(n // tm,),
        in_specs=[pl.BlockSpec((tm, d), lambda i: (i, 0)), full((1, d)), full(w_in.shape),
                  full((1, qk_w)), full((qk_w, qk_w)), tab_spec, tab_spec, tab_spec],
        out_specs=[pl.BlockSpec((tm, q_w), lambda i: (i, 0)), pl.BlockSpec((tm, kv_w), lambda i: (i, 0)),
                   pl.BlockSpec((tm, kv_w), lambda i: (i, 0)), pl.BlockSpec((tm, b_w), lambda i: (i, 0))],
        out_shape=[jax.ShapeDtypeStruct((n, q_w), BF16), jax.ShapeDtypeStruct((n, kv_w), BF16),
                   jax.ShapeDtypeStruct((n, kv_w), BF16), jax.ShapeDtypeStruct((n, b_w), F32)],
        compiler_params=_cparams("parallel"),
        name="even_in_proj",
    )(x2d, g[None, :].astype(F32), w_in.astype(BF16), gqk, bd, cos_t, s1_t, s2_t)


def _gqa_kernel(qT_ref, k_ref, vT_ref, o_ref, *, nk):
    qT = qT_ref[0, 0]
    hd, tq = qT.shape

    def body(j, carry):
        m, l, acc = carry
        s = jnp.dot(k_ref[0, 0, j], qT, preferred_element_type=F32)
        m_new = jnp.maximum(m, jnp.max(s, axis=0, keepdims=True))
        alpha = jnp.exp2(m - m_new)
        p = jnp.exp2(s - m_new)
        l = alpha * l + jnp.sum(p, axis=0, keepdims=True)
        acc = alpha * acc + jnp.dot(vT_ref[0, 0, j], p.astype(BF16), preferred_element_type=F32)
        return m_new, l, acc

    init = (jnp.full((1, tq), NEG_BIG, F32), jnp.zeros((1, tq), F32), jnp.zeros((hd, tq), F32))
    _, l, acc = lax.fori_loop(0, nk, body, init)
    o_ref[0, 0] = (acc / l).astype(o_ref.dtype)


def _gqa_attention(q, k, v, batch, seq):
    hd, tq, tk = HEAD_DIM, TQ_A, TK_A
    nk = seq // tk
    qT = q.reshape(batch, seq, A_HEADS, hd).transpose(0, 2, 3, 1)
    kc = k.reshape(batch, nk, tk, A_KV_HEADS, hd).transpose(0, 3, 1, 2, 4)
    vT = v.reshape(batch, nk, tk, A_KV_HEADS, hd).transpose(0, 3, 1, 4, 2)
    oT = pl.pallas_call(
        functools.partial(_gqa_kernel, nk=nk),
        grid=(batch, A_HEADS, seq // tq),
        in_specs=[pl.BlockSpec((1, 1, hd, tq), lambda b, h, i: (b, h, 0, i)),
                  pl.BlockSpec((1, 1, nk, tk, hd), lambda b, h, i: (b, h // A_GROUP, 0, 0, 0)),
                  pl.BlockSpec((1, 1, nk, hd, tk), lambda b, h, i: (b, h // A_GROUP, 0, 0, 0))],
        out_specs=pl.BlockSpec((1, 1, hd, tq), lambda b, h, i: (b, h, 0, i)),
        out_shape=jax.ShapeDtypeStruct((batch, A_HEADS, hd, seq), BF16),
        compiler_params=_cparams("parallel", "parallel", "parallel"),
        name="gqa_attention",
    )(qT, kc, vT)
    return oT.transpose(0, 3, 1, 2).reshape(batch * seq, A_HEADS * hd)


def _conv_kernel(prev_ref, cur_ref, next_ref, w_ref, b_ref, g_ref, beta_ref, bd_ref, o_ref, xe_ref, *, ts):
    i = pl.program_id(1)
    last = pl.num_programs(1) - 1
    halo = CONV_HALO
    xe_ref[0:halo, :] = jnp.where(i > 0, prev_ref[0], 0.0)
    xe_ref[halo:halo + ts, :] = cur_ref[0]
    xe_ref[halo + ts:halo + ts + halo, :] = jnp.where(i < last, next_ref[0], 0.0)
    w = w_ref[...]
    first = halo - CONV_WIDTH // 2
    for r0 in range(0, ts, CONV_SUB):
        acc = jnp.zeros((CONV_SUB, w.shape[1]), F32) + b_ref[...]
        for j in range(CONV_WIDTH):
            acc = acc + w[j:j + 1, :] * xe_ref[r0 + first + j:r0 + first + j + CONV_SUB, :]
        mu = _split_dot(acc, bd_ref[...])
        cen = acc - mu
        var = _split_dot(cen * cen, bd_ref[...])
        hn = cen * lax.rsqrt(var + 1e-5) * g_ref[...] + beta_ref[...]
        o_ref[0, r0:r0 + CONV_SUB, :] = (hn * _sigmoid(hn)).astype(o_ref.dtype)


def _conformer_conv(glu, conv_w, conv_b, gn_g, gn_b, batch, seq):
    c = glu.shape[1]
    ts, halo = TS_CONV, CONV_HALO
    x3 = glu.reshape(batch, seq, c)
    hpb = ts // halo
    nh = seq // halo
    gid = jnp.arange(c) // B_GROUP_WIDTH
    bd = (gid[:, None] == gid[None, :]).astype(BF16) * (1.0 / B_GROUP_WIDTH)
    wpad = jnp.zeros((32, c), F32).at[:CONV_WIDTH].set(conv_w.astype(F32))
    vec = lambda a: a[None, :].astype(F32)
    full = lambda shape: pl.BlockSpec(shape, lambda b, i: (0,) * len(shape))
    out = pl.pallas_call(
        functools.partial(_conv_kernel, ts=ts),
        grid=(batch, seq // ts),
        in_specs=[pl.BlockSpec((1, halo, c), lambda b, i: (b, jnp.maximum(i * hpb - 1, 0), 0)),
                  pl.BlockSpec((1, ts, c), lambda b, i: (b, i, 0)),
                  pl.BlockSpec((1, halo, c), lambda b, i: (b, jnp.minimum((i + 1) * hpb, nh - 1), 0)),
                  full((32, c)), full((1, c)), full((1, c)), full((1, c)), full((c, c))],
        out_specs=pl.BlockSpec((1, ts, c), lambda b, i: (b, i, 0)),
        out_shape=jax.ShapeDtypeStruct((batch, seq, c), BF16),
        scratch_shapes=[pltpu.VMEM((ts + 2 * halo, c), F32)],
        compiler_params=_cparams("parallel", "parallel"),
        name="conformer_conv",
    )(x3, x3, x3, wpad, vec(conv_b), vec(gn_g), vec(gn_b), bd)
    return out.reshape(batch * seq, c)


def _route(logits):
    lane = lax.broadcasted_iota(jnp.int32, logits.shape, 1)
    big = jnp.int32(1 << 20)
    ninf = jnp.float32(-jnp.inf)
    gmask = (lane >= MOE_EXPERTS) & (lane < MOE_EXPERTS + MOE_GROUPS)
    lg = jnp.where(gmask, logits, ninf)
    mg = jnp.max(lg, axis=-1, keepdims=True)
    sg = jnp.sum(jnp.exp(lg - mg), axis=-1, keepdims=True)
    g_val = 1.0 / sg
    g_idx = jnp.min(jnp.where(lg == mg, lane - MOE_EXPERTS, big), axis=-1, keepdims=True)
    emask = (lane < MOE_EXPERTS) & ((lane // MOE_PER_GROUP) == g_idx)
    le = jnp.where(emask, logits, ninf)
    m1 = jnp.max(le, axis=-1, keepdims=True)
    se = jnp.sum(jnp.exp(le - m1), axis=-1, keepdims=True)
    i1 = jnp.min(jnp.where(le == m1, lane, big), axis=-1, keepdims=True)
    le2 = jnp.where(lane == i1, ninf, le)
    m2 = jnp.max(le2, axis=-1, keepdims=True)
    i2 = jnp.min(jnp.where(le2 == m2, lane, big), axis=-1, keepdims=True)
    e1 = 1.0 / se
    e2 = jnp.exp(m2 - m1) / se
    norm = g_val / (e1 + e2)
    ids = jnp.where(lane == 0, i1, jnp.where(lane == 1, i2, 0))
    gates = jnp.where(lane == 0, e1 * norm, jnp.where(lane == 1, e2 * norm, 0.0))
    return ids, gates


def _out_router_kernel(ya_ref, yb_ref, x_ref, wa_ref, wb_ref, g_ref, wrh_ref, wrl_ref, br_ref,
                       x1_ref, h_ref, ids_ref, gates_ref):
    x1 = (x_ref[...] + jnp.dot(ya_ref[...], wa_ref[...], preferred_element_type=F32)
          + jnp.dot(yb_ref[...], wb_ref[...], preferred_element_type=F32))
    x1_ref[...] = x1
    h = _rms_rows(x1, g_ref[...], EPS)
    h_ref[...] = h
    hi = h.astype(BF16)
    lo = (h - hi.astype(F32)).astype(BF16)
    logits = (jnp.dot(hi, wrh_ref[...], preferred_element_type=F32)
              + jnp.dot(hi, wrl_ref[...], preferred_element_type=F32)
              + jnp.dot(lo, wrh_ref[...], preferred_element_type=F32) + br_ref[...])
    ids, gates = _route(logits)
    ids_ref[...] = ids
    gates_ref[...] = gates


def _out_router(y, split, x2d, w_out, g_ffn, w_group, b_group, w_expert, b_expert):
    n, d = x2d.shape
    tm = TM_PROJ
    half = w_out.shape[0] // 2
    if split:
        ya, yb = y
        ya_spec = pl.BlockSpec((tm, half), lambda i: (i, 0))
        yb_spec = pl.BlockSpec((tm, half), lambda i: (i, 0))
    else:
        ya = yb = y
        ya_spec = pl.BlockSpec((tm, half), lambda i: (i, 0))
        yb_spec = pl.BlockSpec((tm, half), lambda i: (i, 1))
    wb16 = w_out.astype(BF16)
    wr = jnp.zeros((d, LANES), F32).at[:, :MOE_EXPERTS].set(w_expert.astype(F32))
    wr = wr.at[:, MOE_EXPERTS:MOE_EXPERTS + MOE_GROUPS].set(w_group.astype(F32))
    wrh = wr.astype(BF16)
    wrl = (wr - wrh.astype(F32)).astype(BF16)
    br = jnp.zeros((1, LANES), F32).at[0, :MOE_EXPERTS].set(b_expert.astype(F32))
    br = br.at[0, MOE_EXPERTS:MOE_EXPERTS + MOE_GROUPS].set(b_group.astype(F32))
    full = lambda shape: pl.BlockSpec(shape, lambda i: (0,) * len(shape))
    row = lambda w: pl.BlockSpec((tm, w), lambda i: (i, 0))
    return pl.pallas_call(
        _out_router_kernel,
        grid=(n // tm,),
        in_specs=[ya_spec, yb_spec, row(d), full((half, d)), full((half, d)), full((1, d)),
                  full((d, LANES)), full((d, LANES)), full((1, LANES))],
        out_specs=[row(d), row(d), row(LANES), row(LANES)],
        out_shape=[jax.ShapeDtypeStruct((n, d), F32), jax.ShapeDtypeStruct((n, d), F32),
                   jax.ShapeDtypeStruct((n, LANES), jnp.int32), jax.ShapeDtypeStruct((n, LANES), F32)],
        compiler_params=_cparams("parallel"),
        name="out_proj_router",
    )(ya, yb, x2d, wb16[:half], wb16[half:], g_ffn[None, :].astype(F32), wrh, wrl, br)


def _moe_dispatch(ids, gates, n):
    bm = MOE_BM
    a = n * MOE_TOPK
    p = a + MOE_EXPERTS * bm
    flat_e = ids.reshape(-1)
    flat_w = gates.reshape(-1)
    order = jnp.argsort(flat_e).astype(jnp.int32)
    se = flat_e[order]
    counts = jnp.bincount(flat_e, length=MOE_EXPERTS).astype(jnp.int32)
    pcounts = (counts + bm - 1) // bm * bm
    starts = jnp.cumsum(counts) - counts
    pends = jnp.cumsum(pcounts)
    pstarts = pends - pcounts
    dest = pstarts[se] + (jnp.arange(a, dtype=jnp.int32) - starts[se])
    tok = order // MOE_TOPK
    slot = order % MOE_TOPK
    valid = jnp.zeros((p,), jnp.int32).at[dest].set(1)
    src_tok = jnp.zeros((p,), jnp.int32).at[dest].set(tok)
    dump = a + jnp.cumsum(1 - valid) - 1
    dst_row = dump.astype(jnp.int32).at[dest].set(slot * n + tok)
    row_w = jnp.zeros((p,), F32).at[dest].set(flat_w[order])
    n_blk = p // bm
    blk_e = jnp.minimum(jnp.searchsorted(pends, jnp.arange(n_blk, dtype=jnp.int32) * bm, side='right'),
                        MOE_EXPERTS - 1).astype(jnp.int32)
    tok_bits = (n - 1).bit_length()
    assert tok_bits + (p - 1).bit_length() <= 32
    packed = lax.bitcast_convert_type(
        src_tok.astype(jnp.uint32) | (dst_row.astype(jnp.uint32) << tok_bits), jnp.int32)
    return blk_e, packed, row_w[:, None], tok_bits


def _moe_kernel(be_ref, idx_ref, h_hbm, w1_ref, w3_ref, w2_ref, rw_ref, y_hbm,
                xbuf, obuf, gsem, ssem, *, tok_bits):
    del be_ref
    bm = MOE_BM
    i = pl.program_id(0)
    nb = pl.num_programs(0)
    slot = i % 2
    tok_mask = (1 << tok_bits) - 1

    def gather_start(blk, sl):
        def body(r, c):
            t = idx_ref[blk * bm + r] & tok_mask
            pltpu.make_async_copy(h_hbm.at[pl.ds(t, 1), :], xbuf.at[sl, pl.ds(r, 1), :], gsem.at[sl]).start()
            return c
        lax.fori_loop(0, bm, body, 0)

    def gather_wait(sl):
        def body(r, c):
            pltpu.make_async_copy(h_hbm.at[pl.ds(0, 1), :], xbuf.at[sl, pl.ds(r, 1), :], gsem.at[sl]).wait()
            return c
        lax.fori_loop(0, bm, body, 0)

    def scatter_start(blk, sl):
        def body(r, c):
            t = lax.shift_right_logical(idx_ref[blk * bm + r], tok_bits)
            pltpu.make_async_copy(obuf.at[sl, pl.ds(r, 1), :], y_hbm.at[pl.ds(t, 1), :], ssem.at[sl]).start()
            return c
        lax.fori_loop(0, bm, body, 0)

    def scatter_wait(sl):
        def body(r, c):
            pltpu.make_async_copy(obuf.at[sl, pl.ds(r, 1), :], y_hbm.at[pl.ds(0, 1), :], ssem.at[sl]).wait()
            return c
        lax.fori_loop(0, bm, body, 0)

    @pl.when(i == 0)
    def _():
        gather_start(0, 0)

    @pl.when(i + 1 < nb)
    def _():
        gather_start(i + 1, 1 - slot)

    gather_wait(slot)
    x = xbuf[slot].astype(BF16)
    h1 = jnp.dot(x, w1_ref[0].astype(BF16), preferred_element_type=F32)
    h3 = jnp.dot(x, w3_ref[0].astype(BF16), preferred_element_type=F32)
    hm = (h1 * _sigmoid(h1) * h3).astype(BF16)
    y = jnp.dot(hm, w2_ref[0].astype(BF16), preferred_element_type=F32) * rw_ref[...]

    @pl.when(i >= 2)
    def _():
        scatter_wait(slot)

    obuf[slot] = y
    scatter_start(i, slot)

    @pl.when(i == nb - 1)
    def _():
        scatter_wait(slot)
        scatter_wait(1 - slot)


def _moe_ffn(h2d, ids, gates, w1, w3, w2):
    n, d = h2d.shape
    bm = MOE_BM
    hid = w1.shape[-1]
    blk_e, packed, row_w, tok_bits = _moe_dispatch(ids, gates, n)
    n_blk = blk_e.shape[0]
    rows_out = n * MOE_TOPK + MOE_EXPERTS * bm
    grid_spec = pltpu.PrefetchScalarGridSpec(
        num_scalar_prefetch=2,
        grid=(n_blk,),
        in_specs=[pl.BlockSpec(memory_space=pl.ANY),
                  pl.BlockSpec((1, d, hid), lambda i, be, ix: (be[i], 0, 0)),
                  pl.BlockSpec((1, d, hid), lambda i, be, ix: (be[i], 0, 0)),
                  pl.BlockSpec((1, hid, d), lambda i, be, ix: (be[i], 0, 0)),
                  pl.BlockSpec((bm, 1), lambda i, be, ix: (i, 0))],
        out_specs=pl.BlockSpec(memory_space=pl.ANY),
        scratch_shapes=[pltpu.VMEM((2, bm, d), F32), pltpu.VMEM((2, bm, d), F32),
                        pltpu.SemaphoreType.DMA((2,)), pltpu.SemaphoreType.DMA((2,))],
    )
    return pl.pallas_call(
        functools.partial(_moe_kernel, tok_bits=tok_bits),
        grid_spec=grid_spec,
        out_shape=jax.ShapeDtypeStruct((rows_out, d), F32),
        compiler_params=_cparams("arbitrary"),
        name="moe_ffn",
    )(blk_e, packed, h2d, w1, w3, w2, row_w)


def _odd_in_kernel(x_ref, y0_ref, y1_ref, g_ref, w_ref, xo_ref, q_ref, k_ref, v_ref, *, cw):
    x = x_ref[...] + y0_ref[...] + y1_ref[...]
    xo_ref[...] = x
    h = _rms_rows(x, g_ref[...], EPS)
    z = jnp.dot(h.astype(BF16), w_ref[...], preferred_element_type=F32)
    q_ref[...] = (z[:, :cw] * (C_HEAD_DIM ** -0.5 * LOG2E)).astype(BF16)
    k_ref[...] = z[:, cw:2 * cw].astype(BF16)
    v_ref[...] = z[:, 2 * cw:3 * cw].astype(BF16)


def _odd_in(x2d, ybuf, g, w_in):
    n, d = x2d.shape
    tm = TM_PROJ
    cw = w_in.shape[1] // 3
    nt = n // tm
    row = lambda w: pl.BlockSpec((tm, w), lambda i: (i, 0))
    full = lambda shape: pl.BlockSpec(shape, lambda i: (0,) * len(shape))
    return pl.pallas_call(
        functools.partial(_odd_in_kernel, cw=cw),
        grid=(nt,),
        in_specs=[row(d), row(d), pl.BlockSpec((tm, d), lambda i: (i + nt, 0)), full((1, d)), full(w_in.shape)],
        out_specs=[row(d), row(cw), row(cw), row(cw)],
        out_shape=[jax.ShapeDtypeStruct((n, d), F32)] + [jax.ShapeDtypeStruct((n, cw), BF16)] * 3,
        compiler_params=_cparams("parallel"),
        name="odd_in_proj",
    )(x2d, ybuf, ybuf, g[None, :].astype(F32), w_in.astype(BF16))


def _t5_bucket(rel):
    nb = REL_BUCKETS // 2
    max_exact = nb // 2
    ret = jnp.where(rel > 0, nb, 0).astype(jnp.int32)
    n = jnp.abs(rel)
    nf = jnp.maximum(n, 1).astype(F32)
    large = max_exact + (jnp.log(nf / max_exact) / math.log(REL_MAX_DIST / max_exact) * (nb - max_exact)).astype(jnp.int32)
    large = jnp.minimum(large, nb - 1)
    return ret + jnp.where(n < max_exact, n, large)


def _diff_kernel(lam_ref, qT_ref, k_ref, vT_ref, band_ref, gsub_ref, o_ref, *, nk):
    i = pl.program_id(2)
    q0 = qT_ref[0, 0, 0]
    q1 = qT_ref[0, 0, 1]
    hd, tq = q0.shape
    dv = vT_ref.shape[3]

    def one_map(q, kc, vc, bias, m, l, acc):
        s = jnp.dot(kc, q, preferred_element_type=F32) + bias
        m_new = jnp.maximum(m, jnp.max(s, axis=0, keepdims=True))
        alpha = jnp.exp2(m - m_new)
        p = jnp.exp2(s - m_new)
        l = alpha * l + jnp.sum(p, axis=0, keepdims=True)
        acc = alpha * acc + jnp.dot(vc, p.astype(BF16), preferred_element_type=F32)
        return m_new, l, acc

    def body(j, carry):
        m0, l0, a0, m1, l1, a1 = carry
        d = jnp.clip(j - i, -2, 2) + 2
        bias = band_ref[0, d]
        vc = vT_ref[0, 0, j]
        m0, l0, a0 = one_map(q0, k_ref[0, 0, 0, j], vc, bias, m0, l0, a0)
        m1, l1, a1 = one_map(q1, k_ref[0, 0, 1, j], vc, bias, m1, l1, a1)
        return m0, l0, a0, m1, l1, a1

    mi = jnp.full((1, tq), NEG_BIG, F32)
    li = jnp.zeros((1, tq), F32)
    ai = jnp.zeros((dv, tq), F32)
    _, l0, a0, _, l1, a1 = lax.fori_loop(0, nk, body, (mi, li, ai, mi, li, ai))
    o = a0 / l0 - lam_ref[...] * (a1 / l1)
    ms = jnp.mean(o * o, axis=0, keepdims=True)
    o_ref[0, 0] = (o * lax.rsqrt(ms + 1e-5) * gsub_ref[...]).astype(o_ref.dtype)


def _diff_attention(q, k, v, lam, lam_init, subln, rel_bias, batch, seq):
    hd, tq, tk = C_HEAD_DIM, TQ_C, TK_C
    dv = 2 * hd
    nk = seq // tk
    qT = q.reshape(batch, seq, C_HEADS, 2, hd).transpose(0, 2, 3, 4, 1)
    kc = k.reshape(batch, nk, tk, C_HEADS, 2, hd).transpose(0, 3, 4, 1, 2, 5)
    vT = v.reshape(batch, nk, tk, C_HEADS, dv).transpose(0, 3, 1, 4, 2)
    assert tk == tq and tk >= REL_MAX_DIST
    dd = jnp.arange(-2, 3, dtype=jnp.int32)[:, None, None] * tk
    rel = dd + jnp.arange(tk, dtype=jnp.int32)[None, :, None] - jnp.arange(tq, dtype=jnp.int32)[None, None, :]
    band = rel_bias.astype(F32)[_t5_bucket(rel)]
    band = band.transpose(3, 0, 1, 2) * LOG2E
    lam_vec = jnp.full((1, tq), lam, F32)
    gsub = (subln.astype(F32) * (1.0 - lam_init))[:, None]
    oT = pl.pallas_call(
        functools.partial(_diff_kernel, nk=nk),
        grid=(batch, C_HEADS, seq // tq),
        in_specs=[pl.BlockSpec((1, tq), lambda b, h, i: (0, 0)),
                  pl.BlockSpec((1, 1, 2, hd, tq), lambda b, h, i: (b, h, 0, 0, i)),
                  pl.BlockSpec((1, 1, 2, nk, tk, hd), lambda b, h, i: (b, h, 0, 0, 0, 0)),
                  pl.BlockSpec((1, 1, nk, dv, tk), lambda b, h, i: (b, h, 0, 0, 0)),
                  pl.BlockSpec((1, 5, tk, tq), lambda b, h, i: (h, 0, 0, 0)),
                  pl.BlockSpec((dv, 1), lambda b, h, i: (0, 0))],
        out_specs=pl.BlockSpec((1, 1, dv, tq), lambda b, h, i: (b, h, 0, i)),
        out_shape=jax.ShapeDtypeStruct((batch, C_HEADS, dv, seq), BF16),
        compiler_params=_cparams("parallel", "parallel", "parallel"),
        name="diff_attention",
    )(lam_vec, qT, kc, vT, band, gsub)
    return oT.transpose(0, 3, 1, 2).reshape(batch * seq, C_HEADS * dv)


def _final_kernel(x_ref, y0_ref, y1_ref, g_ref, o_ref):
    x = x_ref[...] + y0_ref[...] + y1_ref[...]
    o_ref[...] = _rms_rows(x, g_ref[...], EPS)


def _final(x2d, ybuf, g):
    n, d = x2d.shape
    tm = TM_PROJ
    nt = n // tm
    row = pl.BlockSpec((tm, d), lambda i: (i, 0))
    return pl.pallas_call(
        _final_kernel,
        grid=(nt,),
        in_specs=[row, row, pl.BlockSpec((tm, d), lambda i: (i + nt, 0)), pl.BlockSpec((1, d), lambda i: (0, 0))],
        out_specs=row,
        out_shape=jax.ShapeDtypeStruct((n, d), F32),
        compiler_params=_cparams("parallel"),
        name="final_norm",
    )(x2d, ybuf, ybuf, g[None, :].astype(F32))


def kernel(x, norm_mix, norm_ffn, norm_final, ev_w_in, ev_q_norm, ev_k_norm, ev_conv_w, ev_conv_b, ev_gn_g, ev_gn_b, ev_w_out, od_w_in, od_lam_q1, od_lam_k1, od_lam_q2, od_lam_k2, od_subln, od_w_out, rel_bias, moe_w_group, moe_b_group, moe_w_expert, moe_b_expert, moe_w1, moe_w3, moe_w2):
    batch, seq, d = x.shape
    n = batch * seq
    x2d = x.reshape(n, d).astype(F32)

    q, k, v, glu = _even_in(x2d, norm_mix[0], ev_w_in[0], ev_q_norm[0], ev_k_norm[0], seq)
    ya = _gqa_attention(q, k, v, batch, seq)
    yb = _conformer_conv(glu, ev_conv_w[0], ev_conv_b[0], ev_gn_g[0], ev_gn_b[0], batch, seq)
    x1, h, ids, gates = _out_router((ya, yb), True, x2d, ev_w_out[0], norm_ffn[0],
                                    moe_w_group[0], moe_b_group[0], moe_w_expert[0], moe_b_expert[0])
    ybuf = _moe_ffn(h, ids[:, :MOE_TOPK], gates[:, :MOE_TOPK], moe_w1[0], moe_w3[0], moe_w2[0])

    x2, q, k, v = _odd_in(x1, ybuf, norm_mix[1], od_w_in[0])
    layer_idx = 1
    lam_init = 0.8 - 0.6 * math.exp(-0.3 * layer_idx)
    lam = (jnp.exp(jnp.sum(od_lam_q1[0].astype(F32) * od_lam_k1[0].astype(F32)))
           - jnp.exp(jnp.sum(od_lam_q2[0].astype(F32) * od_lam_k2[0].astype(F32))) + lam_init)
    o = _diff_attention(q, k, v, lam, lam_init, od_subln[0], rel_bias, batch, seq)
    x3, h, ids, gates = _out_router(o, False, x2, od_w_out[0], norm_ffn[1],
                                    moe_w_group[1], moe_b_group[1], moe_w_expert[1], moe_b_expert[1])
    ybuf = _moe_ffn(h, ids[:, :MOE_TOPK], gates[:, :MOE_TOPK], moe_w1[1], moe_w3[1], moe_w2[1])
    out = _final(x3, ybuf, norm_final)
    return out.reshape(batch, seq, d)
```

```python
import functools
import math

import jax
import jax.numpy as jnp
from jax import lax
from jax.experimental import pallas as pl
from jax.experimental.pallas import tpu as pltpu

F32 = jnp.float32
BF16 = jnp.bfloat16

GRID_W = 64
HEAD_DIM = 64
EPS = 1e-6
A_HEADS = 8
A_KV_HEADS = 2
A_GROUP = A_HEADS // A_KV_HEADS
ROPE_THETA = 10000.0
ROPE_HALF = HEAD_DIM // 4
B_GROUP_WIDTH = 64
CONV_WIDTH = 31
CONV_HALO = 16
C_HEADS = 8
C_HEAD_DIM = 64
REL_BUCKETS = 32
REL_MAX_DIST = 128
MOE_GROUPS = 4
MOE_PER_GROUP = 8
MOE_EXPERTS = MOE_GROUPS * MOE_PER_GROUP
MOE_TOPK = 2
LOG2E = 1.4426950408889634
NEG_BIG = -1e30
LANES = 128

TM_PROJ = 512
TQ_A = 512
TK_A = 512
HEADS_PER_STEP_A = 2
TQ_C = 512
TK_C = 512
TS_CONV = 256
CONV_SUB = 64
MOE_BM = 128
VMEM_LIMIT = 56 * 1024 * 1024


def _cparams(*sem):
    return pltpu.CompilerParams(dimension_semantics=sem, vmem_limit_bytes=VMEM_LIMIT)


def _split_dot(a, b_bf16):
    hi = a.astype(BF16)
    lo = (a - hi.astype(F32)).astype(BF16)
    return (jnp.dot(hi, b_bf16, preferred_element_type=F32)
            + jnp.dot(lo, b_bf16, preferred_element_type=F32))


def _rms_rows(x, g, eps):
    ms = jnp.mean(x * x, axis=-1, keepdims=True)
    return x * lax.rsqrt(ms + eps) * g


def _sigmoid(x):
    return 1.0 / (1.0 + jnp.exp(-x))


def _even_in_kernel(x_ref, g_ref, w_ref, gqk_ref, bd_ref, cos_ref, s1_ref, s2_ref,
                    q_ref, k_ref, v_ref, glu_ref, *, qk_w, q_w, kv_w, b_w):
    h = _rms_rows(x_ref[...], g_ref[...], EPS)
    z = jnp.dot(h.astype(BF16), w_ref[...], preferred_element_type=F32)
    qk = z[:, :qk_w]
    ms = _split_dot(qk * qk, bd_ref[...])
    qkn = qk * lax.rsqrt(ms + EPS) * gqk_ref[...]
    cos, s1, s2 = cos_ref[...], s1_ref[...], s2_ref[...]
    parts = []
    for c in range(qk_w // LANES):
        blk = qkn[:, c * LANES:(c + 1) * LANES]
        up = pltpu.roll(blk, LANES - ROPE_HALF, 1)
        dn = pltpu.roll(blk, ROPE_HALF, 1)
        parts.append(blk * cos + up * s1 + dn * s2)
    rot = jnp.concatenate(parts, axis=1)
    q_ref[...] = rot[:, :q_w].astype(BF16)
    k_ref[...] = rot[:, q_w:qk_w].astype(BF16)
    v_ref[...] = z[:, qk_w:qk_w + kv_w].astype(BF16)
    a = z[:, qk_w + kv_w:qk_w + kv_w + b_w]
    gate = z[:, qk_w + kv_w + b_w:qk_w + kv_w + 2 * b_w]
    glu_ref[...] = a * _sigmoid(gate)


def _even_in(x2d, g, w_in, q_norm, k_norm, seq):
    n, d = x2d.shape
    q_w, kv_w, b_w = A_HEADS * HEAD_DIM, A_KV_HEADS * HEAD_DIM, d // 2
    qk_w = q_w + kv_w
    tm = TM_PROJ
    gq = jnp.tile(q_norm.astype(F32), A_HEADS) * (HEAD_DIM ** -0.5 * LOG2E)
    gk = jnp.tile(k_norm.astype(F32), A_KV_HEADS)
    gqk = jnp.concatenate([gq, gk])[None, :]
    hid = jnp.arange(qk_w) // HEAD_DIM
    bd = (hid[:, None] == hid[None, :]).astype(BF16) * (1.0 / HEAD_DIM)
    pos = jnp.arange(seq, dtype=jnp.int32)
    row, col = pos // GRID_W, pos % GRID_W
    m = ROPE_HALF
    inv = ROPE_THETA ** (-jnp.arange(m, dtype=F32) / m)
    ang_r = row.astype(F32)[:, None] * inv[None, :]
    ang_c = col.astype(F32)[:, None] * inv[None, :]
    zero = jnp.zeros_like(ang_r)
    cos64 = jnp.concatenate([jnp.cos(ang_r), jnp.cos(ang_r), jnp.cos(ang_c), jnp.cos(ang_c)], axis=1)
    s1_64 = jnp.concatenate([-jnp.sin(ang_r), zero, -jnp.sin(ang_c), zero], axis=1)
    s2_64 = jnp.concatenate([zero, jnp.sin(ang_r), zero, jnp.sin(ang_c)], axis=1)
    rep = LANES // HEAD_DIM
    cos_t, s1_t, s2_t = (jnp.tile(t, (1, rep)) for t in (cos64, s1_64, s2_64))
    nseq = seq // tm
    tab_spec = pl.BlockSpec((tm, LANES), lambda i: (i % nseq, 0))
    full = lambda shape: pl.BlockSpec(shape, lambda i: (0,) * len(shape))
    kern = functools.partial(_even_in_kernel, qk_w=qk_w, q_w=q_w, kv_w=kv_w, b_w=b_w)
    return pl.pallas_call(
        kern,
        grid=(n // tm,),
        in_specs=[pl.BlockSpec((tm, d), lambda i: (i, 0)), full((1, d)), full(w_in.shape),
                  full((1, qk_w)), full((qk_w, qk_w)), tab_spec, tab_spec, tab_spec],
        out_specs=[pl.BlockSpec((tm, q_w), lambda i: (i, 0)), pl.BlockSpec((tm, kv_w), lambda i: (i, 0)),
                   pl.BlockSpec((tm, kv_w), lambda i: (i, 0)), pl.BlockSpec((tm, b_w), lambda i: (i, 0))],
        out_shape=[jax.ShapeDtypeStruct((n, q_w), BF16), jax.ShapeDtypeStruct((n, kv_w), BF16),
                   jax.ShapeDtypeStruct((n, kv_w), BF16), jax.ShapeDtypeStruct((n, b_w), F32)],
        compiler_params=_cparams("parallel"),
        name="even_in_proj",
    )(x2d, g[None, :].astype(F32), w_in.astype(BF16), gqk, bd, cos_t, s1_t, s2_t)


def _flash_pipelined(nk, n_maps, score_fn, pv_fn, s_a, s_b, tq, dv):
    assert nk % 2 == 0 and nk >= 2

    def scores(j, s_ref):
        cms = []
        for c in range(n_maps):
            s = score_fn(j, c)
            s_ref[c] = s
            cms.append(jnp.max(s, axis=0, keepdims=True))
        return tuple(cms)

    def softmax(j, s_ref, cms, state):
        new = []
        for c in range(n_maps):
            m, l, acc = state[c]
            m_new = jnp.maximum(m, cms[c])
            alpha = jnp.exp2(m - m_new)
            p = jnp.exp2(s_ref[c] - m_new)
            l = alpha * l + jnp.sum(p, axis=0, keepdims=True)
            acc = alpha * acc + pv_fn(j, c, p.astype(BF16))
            new.append((m_new, l, acc))
        return tuple(new)

    init = tuple((jnp.full((1, tq), NEG_BIG, F32), jnp.zeros((1, tq), F32), jnp.zeros((dv, tq), F32))
                 for _ in range(n_maps))
    cms0 = scores(0, s_a)

    def pair(t, carry):
        cms_a, state = carry
        j = 2 * t
        cms_b = scores(j + 1, s_b)
        state = softmax(j, s_a, cms_a, state)
        cms_a = scores(j + 2, s_a)
        state = softmax(j + 1, s_b, cms_b, state)
        return cms_a, state

    cms_a, state = lax.fori_loop(0, nk // 2 - 1, pair, (cms0, init))
    cms_b = scores(nk - 1, s_b)
    state = softmax(nk - 2, s_a, cms_a, state)
    state = softmax(nk - 1, s_b, cms_b, state)
    return tuple((l, acc) for _, l, acc in state)


def _gqa_kernel(qT_ref, k_ref, vT_ref, o_ref, s_a, s_b, *, nk):
    nh, hd, tq = qT_ref.shape[1:]
    qs = [qT_ref[0, h] for h in range(nh)]
    score_fn = lambda j, c: jnp.dot(k_ref[0, 0, j], qs[c], preferred_element_type=F32)
    pv_fn = lambda j, c, p: jnp.dot(vT_ref[0, 0, j], p, preferred_element_type=F32)
    res = _flash_pipelined(nk, nh, score_fn, pv_fn, s_a, s_b, tq, hd)
    for h, (l, acc) in enumerate(res):
        o_ref[0, h] = (acc / l).astype(o_ref.dtype)


def _gqa_attention(q, k, v, batch, seq):
    hd, tq, tk, nh = HEAD_DIM, TQ_A, TK_A, HEADS_PER_STEP_A
    nk = seq // tk
    hsteps = A_HEADS // nh
    per_kv = A_GROUP // nh
    qT = q.reshape(batch, seq, A_HEADS, hd).transpose(0, 2, 3, 1)
    kc = k.reshape(batch, nk, tk, A_KV_HEADS, hd).transpose(0, 3, 1, 2, 4)
    vT = v.reshape(batch, nk, tk, A_KV_HEADS, hd).transpose(0, 3, 1, 4, 2)
    oT = pl.pallas_call(
        functools.partial(_gqa_kernel, nk=nk),
        grid=(batch, hsteps, seq // tq),
        in_specs=[pl.BlockSpec((1, nh, hd, tq), lambda b, h, i: (b, h, 0, i)),
                  pl.BlockSpec((1, 1, nk, tk, hd), lambda b, h, i: (b, h // per_kv, 0, 0, 0)),
                  pl.BlockSpec((1, 1, nk, hd, tk), lambda b, h, i: (b, h // per_kv, 0, 0, 0))],
        out_specs=pl.BlockSpec((1, nh, hd, tq), lambda b, h, i: (b, h, 0, i)),
        out_shape=jax.ShapeDtypeStruct((batch, A_HEADS, hd, seq), BF16),
        scratch_shapes=[pltpu.VMEM((nh, tk, tq), F32), pltpu.VMEM((nh, tk, tq), F32)],
        compiler_params=_cparams("parallel", "parallel", "parallel"),
        name="gqa_attention",
    )(qT, kc, vT)
    return oT.transpose(0, 3, 1, 2).reshape(batch * seq, A_HEADS * hd)


def _conv_kernel(prev_ref, cur_ref, next_ref, w_ref, b_ref, g_ref, beta_ref, bd_ref, o_ref, xe_ref, *, ts):
    i = pl.program_id(1)
    last = pl.num_programs(1) - 1
    halo = CONV_HALO
    xe_ref[0:halo, :] = jnp.where(i > 0, prev_ref[0], 0.0)
    xe_ref[halo:halo + ts, :] = cur_ref[0]
    xe_ref[halo + ts:halo + ts + halo, :] = jnp.where(i < last, next_ref[0], 0.0)
    w = w_ref[...]
    first = halo - CONV_WIDTH // 2
    for r0 in range(0, ts, CONV_SUB):
        acc = jnp.zeros((CONV_SUB, w.shape[1]), F32) + b_ref[...]
        for j in range(CONV_WIDTH):
            acc = acc + w[j:j + 1, :] * xe_ref[r0 + first + j:r0 + first + j + CONV_SUB, :]
        mu = _split_dot(acc, bd_ref[...])
        cen = acc - mu
        var = _split_dot(cen * cen, bd_ref[...])
        hn = cen * lax.rsqrt(var + 1e-5) * g_ref[...] + beta_ref[...]
        o_ref[0, r0:r0 + CONV_SUB, :] = (hn * _sigmoid(hn)).astype(o_ref.dtype)


def _conformer_conv(glu, conv_w, conv_b, gn_g, gn_b, batch, seq):
    c = glu.shape[1]
    ts, halo = TS_CONV, CONV_HALO
    x3 = glu.reshape(batch, seq, c)
    hpb = ts // halo
    nh = seq // halo
    gid = jnp.arange(c) // B_GROUP_WIDTH
    bd = (gid[:, None] == gid[None, :]).astype(BF16) * (1.0 / B_GROUP_WIDTH)
    wpad = jnp.zeros((32, c), F32).at[:CONV_WIDTH].set(conv_w.astype(F32))
    vec = lambda a: a[None, :].astype(F32)
    full = lambda shape: pl.BlockSpec(shape, lambda b, i: (0,) * len(shape))
    out = pl.pallas_call(
        functools.partial(_conv_kernel, ts=ts),
        grid=(batch, seq // ts),
        in_specs=[pl.BlockSpec((1, halo, c), lambda b, i: (b, jnp.maximum(i * hpb - 1, 0), 0)),
                  pl.BlockSpec((1, ts, c), lambda b, i: (b, i, 0)),
                  pl.BlockSpec((1, halo, c), lambda b, i: (b, jnp.minimum((i + 1) * hpb, nh - 1), 0)),
                  full((32, c)), full((1, c)), full((1, c)), full((1, c)), full((c, c))],
        out_specs=pl.BlockSpec((1, ts, c), lambda b, i: (b, i, 0)),
        out_shape=jax.ShapeDtypeStruct((batch, seq, c), BF16),
        scratch_shapes=[pltpu.VMEM((ts + 2 * halo, c), F32)],
        compiler_params=_cparams("parallel", "parallel"),
        name="conformer_conv",
    )(x3, x3, x3, wpad, vec(conv_b), vec(gn_g), vec(gn_b), bd)
    return out.reshape(batch * seq, c)


def _route(logits):
    lane = lax.broadcasted_iota(jnp.int32, logits.shape, 1)
    big = jnp.int32(1 << 20)
    ninf = jnp.float32(-jnp.inf)
    gmask = (lane >= MOE_EXPERTS) & (lane < MOE_EXPERTS + MOE_GROUPS)
    lg = jnp.where(gmask, logits, ninf)
    mg = jnp.max(lg, axis=-1, keepdims=True)
    sg = jnp.sum(jnp.exp(lg - mg), axis=-1, keepdims=True)
    g_val = 1.0 / sg
    g_idx = jnp.min(jnp.where(lg == mg, lane - MOE_EXPERTS, big), axis=-1, keepdims=True)
    emask = (lane < MOE_EXPERTS) & ((lane // MOE_PER_GROUP) == g_idx)
    le = jnp.where(emask, logits, ninf)
    m1 = jnp.max(le, axis=-1, keepdims=True)
    se = jnp.sum(jnp.exp(le - m1), axis=-1, keepdims=True)
    i1 = jnp.min(jnp.where(le == m1, lane, big), axis=-1, keepdims=True)
    le2 = jnp.where(lane == i1, ninf, le)
    m2 = jnp.max(le2, axis=-1, keepdims=True)
    i2 = jnp.min(jnp.where(le2 == m2, lane, big), axis=-1, keepdims=True)
    e1 = 1.0 / se
    e2 = jnp.exp(m2 - m1) / se
    norm = g_val / (e1 + e2)
    ids = jnp.where(lane == 0, i1, jnp.where(lane == 1, i2, 0))
    gates = jnp.where(lane == 0, e1 * norm, jnp.where(lane == 1, e2 * norm, 0.0))
    return ids, gates


def _out_router_kernel(ya_ref, yb_ref, x_ref, wa_ref, wb_ref, g_ref, wrh_ref, wrl_ref, br_ref,
                       x1_ref, h_ref, ids_ref, gates_ref):
    x1 = (x_ref[...] + jnp.dot(ya_ref[...], wa_ref[...], preferred_element_type=F32)
          + jnp.dot(yb_ref[...], wb_ref[...], preferred_element_type=F32))
    x1_ref[...] = x1
    h = _rms_rows(x1, g_ref[...], EPS)
    h_ref[...] = h
    hi = h.astype(BF16)
    lo = (h - hi.astype(F32)).astype(BF16)
    logits = (jnp.dot(hi, wrh_ref[...], preferred_element_type=F32)
              + jnp.dot(hi, wrl_ref[...], preferred_element_type=F32)
              + jnp.dot(lo, wrh_ref[...], preferred_element_type=F32) + br_ref[...])
    ids, gates = _route(logits)
    ids_ref[...] = ids
    gates_ref[...] = gates


def _out_router(y, split, x2d, w_out, g_ffn, w_group, b_group, w_expert, b_expert):
    n, d = x2d.shape
    tm = TM_PROJ
    half = w_out.shape[0] // 2
    if split:
        ya, yb = y
        ya_spec = pl.BlockSpec((tm, half), lambda i: (i, 0))
        yb_spec = pl.BlockSpec((tm, half), lambda i: (i, 0))
    else:
        ya = yb = y
        ya_spec = pl.BlockSpec((tm, half), lambda i: (i, 0))
        yb_spec = pl.BlockSpec((tm, half), lambda i: (i, 1))
    wb16 = w_out.astype(BF16)
    wr = jnp.zeros((d, LANES), F32).at[:, :MOE_EXPERTS].set(w_expert.astype(F32))
    wr = wr.at[:, MOE_EXPERTS:MOE_EXPERTS + MOE_GROUPS].set(w_group.astype(F32))
    wrh = wr.astype(BF16)
    wrl = (wr - wrh.astype(F32)).astype(BF16)
    br = jnp.zeros((1, LANES), F32).at[0, :MOE_EXPERTS].set(b_expert.astype(F32))
    br = br.at[0, MOE_EXPERTS:MOE_EXPERTS + MOE_GROUPS].set(b_group.astype(F32))
    full = lambda shape: pl.BlockSpec(shape, lambda i: (0,) * len(shape))
    row = lambda w: pl.BlockSpec((tm, w), lambda i: (i, 0))
    return pl.pallas_call(
        _out_router_kernel,
        grid=(n // tm,),
        in_specs=[ya_spec, yb_spec, row(d), full((half, d)), full((half, d)), full((1, d)),
                  full((d, LANES)), full((d, LANES)), full((1, LANES))],
        out_specs=[row(d), row(d), row(LANES), row(LANES)],
        out_shape=[jax.ShapeDtypeStruct((n, d), F32), jax.ShapeDtypeStruct((n, d), F32),
                   jax.ShapeDtypeStruct((n, LANES), jnp.int32), jax.ShapeDtypeStruct((n, LANES), F32)],
        compiler_params=_cparams("parallel"),
        name="out_proj_router",
    )(ya, yb, x2d, wb16[:half], wb16[half:], g_ffn[None, :].astype(F32), wrh, wrl, br)


def _moe_dispatch(ids, gates, n):
    bm = MOE_BM
    a = n * MOE_TOPK
    p = a + MOE_EXPERTS * bm
    flat_e = ids.reshape(-1)
    flat_w = gates.reshape(-1)
    order = jnp.argsort(flat_e).astype(jnp.int32)
    se = flat_e[order]
    counts = jnp.bincount(flat_e, length=MOE_EXPERTS).astype(jnp.int32)
    pcounts = (counts + bm - 1) // bm * bm
    starts = jnp.cumsum(counts) - counts
    pends = jnp.cumsum(pcounts)
    pstarts = pends - pcounts
    dest = pstarts[se] + (jnp.arange(a, dtype=jnp.int32) - starts[se])
    tok = order // MOE_TOPK
    slot = order % MOE_TOPK
    valid = jnp.zeros((p,), jnp.int32).at[dest].set(1)
    src_tok = jnp.zeros((p,), jnp.int32).at[dest].set(tok)
    dump = a + jnp.cumsum(1 - valid) - 1
    dst_row = dump.astype(jnp.int32).at[dest].set(slot * n + tok)
    row_w = jnp.zeros((p,), F32).at[dest].set(flat_w[order])
    n_blk = p // bm
    blk_e = jnp.minimum(jnp.searchsorted(pends, jnp.arange(n_blk, dtype=jnp.int32) * bm, side='right'),
                        MOE_EXPERTS - 1).astype(jnp.int32)
    tok_bits = (n - 1).bit_length()
    assert tok_bits + (p - 1).bit_length() <= 32
    packed = lax.bitcast_convert_type(
        src_tok.astype(jnp.uint32) | (dst_row.astype(jnp.uint32) << tok_bits), jnp.int32)
    return blk_e, packed, row_w[:, None], tok_bits


def _moe_kernel(be_ref, idx_ref, h_hbm, w1_ref, w3_ref, w2_ref, rw_ref, y_hbm,
                xbuf, obuf, gsem, ssem, *, tok_bits):
    del be_ref
    bm = MOE_BM
    i = pl.program_id(0)
    nb = pl.num_programs(0)
    slot = i % 2
    tok_mask = (1 << tok_bits) - 1

    def gather_start(blk, sl):
        def body(r, c):
            t = idx_ref[blk * bm + r] & tok_mask
            pltpu.make_async_copy(h_hbm.at[pl.ds(t, 1), :], xbuf.at[sl, pl.ds(r, 1), :], gsem.at[sl]).start()
            return c
        lax.fori_loop(0, bm, body, 0)

    def gather_wait(sl):
        def body(r, c):
            pltpu.make_async_copy(h_hbm.at[pl.ds(0, 1), :], xbuf.at[sl, pl.ds(r, 1), :], gsem.at[sl]).wait()
            return c
        lax.fori_loop(0, bm, body, 0)

    def scatter_start(blk, sl):
        def body(r, c):
            t = lax.shift_right_logical(idx_ref[blk * bm + r], tok_bits)
            pltpu.make_async_copy(obuf.at[sl, pl.ds(r, 1), :], y_hbm.at[pl.ds(t, 1), :], ssem.at[sl]).start()
            return c
        lax.fori_loop(0, bm, body, 0)

    def scatter_wait(sl):
        def body(r, c):
            pltpu.make_async_copy(obuf.at[sl, pl.ds(r, 1), :], y_hbm.at[pl.ds(0, 1), :], ssem.at[sl]).wait()
            return c
        lax.fori_loop(0, bm, body, 0)

    @pl.when(i == 0)
    def _():
        gather_start(0, 0)

    @pl.when(i + 1 < nb)
    def _():
        gather_start(i + 1, 1 - slot)

    gather_wait(slot)
    x = xbuf[slot].astype(BF16)
    h1 = jnp.dot(x, w1_ref[0].astype(BF16), preferred_element_type=F32)
    h3 = jnp.dot(x, w3_ref[0].astype(BF16), preferred_element_type=F32)
    hm = (h1 * _sigmoid(h1) * h3).astype(BF16)
    y = jnp.dot(hm, w2_ref[0].astype(BF16), preferred_element_type=F32) * rw_ref[...]

    @pl.when(i >= 2)
    def _():
        scatter_wait(slot)

    obuf[slot] = y
    scatter_start(i, slot)

    @pl.when(i == nb - 1)
    def _():
        scatter_wait(slot)
        scatter_wait(1 - slot)


def _moe_ffn(h2d, ids, gates, w1, w3, w2):
    n, d = h2d.shape
    bm = MOE_BM
    hid = w1.shape[-1]
    blk_e, packed, row_w, tok_bits = _moe_dispatch(ids, gates, n)
    n_blk = blk_e.shape[0]
    rows_out = n * MOE_TOPK + MOE_EXPERTS * bm
    grid_spec = pltpu.PrefetchScalarGridSpec(
        num_scalar_prefetch=2,
        grid=(n_blk,),
        in_specs=[pl.BlockSpec(memory_space=pl.ANY),
                  pl.BlockSpec((1, d, hid), lambda i, be, ix: (be[i], 0, 0)),
                  pl.BlockSpec((1, d, hid), lambda i, be, ix: (be[i], 0, 0)),
                  pl.BlockSpec((1, hid, d), lambda i, be, ix: (be[i], 0, 0)),
                  pl.BlockSpec((bm, 1), lambda i, be, ix: (i, 0))],
        out_specs=pl.BlockSpec(memory_space=pl.ANY),
        scratch_shapes=[pltpu.VMEM((2, bm, d), F32), pltpu.VMEM((2, bm, d), F32),
                        pltpu.SemaphoreType.DMA((2,)), pltpu.SemaphoreType.DMA((2,))],
    )
    return pl.pallas_call(
        functools.partial(_moe_kernel, tok_bits=tok_bits),
        grid_spec=grid_spec,
        out_shape=jax.ShapeDtypeStruct((rows_out, d), F32),
        compiler_params=_cparams("arbitrary"),
        name="moe_ffn",
    )(blk_e, packed, h2d, w1, w3, w2, row_w)


def _odd_in_kernel(x_ref, y0_ref, y1_ref, g_ref, w_ref, xo_ref, q_ref, k_ref, v_ref, *, cw):
    x = x_ref[...] + y0_ref[...] + y1_ref[...]
    xo_ref[...] = x
    h = _rms_rows(x, g_ref[...], EPS)
    z = jnp.dot(h.astype(BF16), w_ref[...], preferred_element_type=F32)
    q_ref[...] = (z[:, :cw] * (C_HEAD_DIM ** -0.5 * LOG2E)).astype(BF16)
    k_ref[...] = z[:, cw:2 * cw].astype(BF16)
    v_ref[...] = z[:, 2 * cw:3 * cw].astype(BF16)


def _odd_in(x2d, ybuf, g, w_in):
    n, d = x2d.shape
    tm = TM_PROJ
    cw = w_in.shape[1] // 3
    nt = n // tm
    row = lambda w: pl.BlockSpec((tm, w), lambda i: (i, 0))
    full = lambda shape: pl.BlockSpec(shape, lambda i: (0,) * len(shape))
    return pl.pallas_call(
        functools.partial(_odd_in_kernel, cw=cw),
        grid=(nt,),
        in_specs=[row(d), row(d), pl.BlockSpec((tm, d), lambda i: (i + nt, 0)), full((1, d)), full(w_in.shape)],
        out_specs=[row(d), row(cw), row(cw), row(cw)],
        out_shape=[jax.ShapeDtypeStruct((n, d), F32)] + [jax.ShapeDtypeStruct((n, cw), BF16)] * 3,
        compiler_params=_cparams("parallel"),
        name="odd_in_proj",
    )(x2d, ybuf, ybuf, g[None, :].astype(F32), w_in.astype(BF16))


def _t5_bucket(rel):
    nb = REL_BUCKETS // 2
    max_exact = nb // 2
    ret = jnp.where(rel > 0, nb, 0).astype(jnp.int32)
    n = jnp.abs(rel)
    nf = jnp.maximum(n, 1).astype(F32)
    large = max_exact + (jnp.log(nf / max_exact) / math.log(REL_MAX_DIST / max_exact) * (nb - max_exact)).astype(jnp.int32)
    large = jnp.minimum(large, nb - 1)
    return ret + jnp.where(n < max_exact, n, large)


def _diff_kernel(lam_ref, qT_ref, k_ref, vT_ref, band_ref, gsub_ref, o_ref, s_a, s_b, *, nk):
    i = pl.program_id(2)
    hd, tq = qT_ref.shape[3:]
    dv = vT_ref.shape[3]
    qs = [qT_ref[0, 0, c] for c in range(2)]

    def score_fn(j, c):
        bias = band_ref[0, jnp.clip(j - i, -2, 2) + 2]
        return jnp.dot(k_ref[0, 0, c, j], qs[c], preferred_element_type=F32) + bias

    pv_fn = lambda j, c, p: jnp.dot(vT_ref[0, 0, j], p, preferred_element_type=F32)
    (l0, a0), (l1, a1) = _flash_pipelined(nk, 2, score_fn, pv_fn, s_a, s_b, tq, dv)
    o = a0 / l0 - lam_ref[...] * (a1 / l1)
    ms = jnp.mean(o * o, axis=0, keepdims=True)
    o_ref[0, 0] = (o * lax.rsqrt(ms + 1e-5) * gsub_ref[...]).astype(o_ref.dtype)


def _diff_attention(q, k, v, lam, lam_init, subln, rel_bias, batch, seq):
    hd, tq, tk = C_HEAD_DIM, TQ_C, TK_C
    dv = 2 * hd
    nk = seq // tk
    qT = q.reshape(batch, seq, C_HEADS, 2, hd).transpose(0, 2, 3, 4, 1)
    kc = k.reshape(batch, nk, tk, C_HEADS, 2, hd).transpose(0, 3, 4, 1, 2, 5)
    vT = v.reshape(batch, nk, tk, C_HEADS, dv).transpose(0, 3, 1, 4, 2)
    assert tk == tq and tk >= REL_MAX_DIST
    dd = jnp.arange(-2, 3, dtype=jnp.int32)[:, None, None] * tk
    rel = dd + jnp.arange(tk, dtype=jnp.int32)[None, :, None] - jnp.arange(tq, dtype=jnp.int32)[None, None, :]
    band = rel_bias.astype(F32)[_t5_bucket(rel)]
    band = band.transpose(3, 0, 1, 2) * LOG2E
    lam_vec = jnp.full((1, tq), lam, F32)
    gsub = (subln.astype(F32) * (1.0 - lam_init))[:, None]
    oT = pl.pallas_call(
        functools.partial(_diff_kernel, nk=nk),
        grid=(batch, C_HEADS, seq // tq),
        in_specs=[pl.BlockSpec((1, tq), lambda b, h, i: (0, 0)),
                  pl.BlockSpec((1, 1, 2, hd, tq), lambda b, h, i: (b, h, 0, 0, i)),
                  pl.BlockSpec((1, 1, 2, nk, tk, hd), lambda b, h, i: (b, h, 0, 0, 0, 0)),
                  pl.BlockSpec((1, 1, nk, dv, tk), lambda b, h, i: (b, h, 0, 0, 0)),
                  pl.BlockSpec((1, 5, tk, tq), lambda b, h, i: (h, 0, 0, 0)),
                  pl.BlockSpec((dv, 1), lambda b, h, i: (0, 0))],
        out_specs=pl.BlockSpec((1, 1, dv, tq), lambda b, h, i: (b, h, 0, i)),
        out_shape=jax.ShapeDtypeStruct((batch, C_HEADS, dv, seq), BF16),
        scratch_shapes=[pltpu.VMEM((2, tk, tq), F32), pltpu.VMEM((2, tk, tq), F32)],
        compiler_params=_cparams("parallel", "parallel", "parallel"),
        name="diff_attention",
    )(lam_vec, qT, kc, vT, band, gsub)
    return oT.transpose(0, 3, 1, 2).reshape(batch * seq, C_HEADS * dv)


def _final_kernel(x_ref, y0_ref, y1_ref, g_ref, o_ref):
    x = x_ref[...] + y0_ref[...] + y1_ref[...]
    o_ref[...] = _rms_rows(x, g_ref[...], EPS)


def _final(x2d, ybuf, g):
    n, d = x2d.shape
    tm = TM_PROJ
    nt = n // tm
    row = pl.BlockSpec((tm, d), lambda i: (i, 0))
    return pl.pallas_call(
        _final_kernel,
        grid=(nt,),
        in_specs=[row, row, pl.BlockSpec((tm, d), lambda i: (i + nt, 0)), pl.BlockSpec((1, d), lambda i: (0, 0))],
        out_specs=row,
        out_shape=jax.ShapeDtypeStruct((n, d), F32),
        compiler_params=_cparams("parallel"),
        name="final_norm",
    )(x2d, ybuf, ybuf, g[None, :].astype(F32))


def kernel(x, norm_mix, norm_ffn, norm_final, ev_w_in, ev_q_norm, ev_k_norm, ev_conv_w, ev_conv_b, ev_gn_g, ev_gn_b, ev_w_out, od_w_in, od_lam_q1, od_lam_k1, od_lam_q2, od_lam_k2, od_subln, od_w_out, rel_bias, moe_w_group, moe_b_group, moe_w_expert, moe_b_expert, moe_w1, moe_w3, moe_w2):
    batch, seq, d = x.shape
    n = batch * seq
    x2d = x.reshape(n, d).astype(F32)

    q, k, v, glu = _even_in(x2d, norm_mix[0], ev_w_in[0], ev_q_norm[0], ev_k_norm[0], seq)
    ya = _gqa_attention(q, k, v, batch, seq)
    yb = _conformer_conv(glu, ev_conv_w[0], ev_conv_b[0], ev_gn_g[0], ev_gn_b[0], batch, seq)
    x1, h, ids, gates = _out_router((ya, yb), True, x2d, ev_w_out[0], norm_ffn[0],
                                    moe_w_group[0], moe_b_group[0], moe_w_expert[0], moe_b_expert[0])
    ybuf = _moe_ffn(h, ids[:, :MOE_TOPK], gates[:, :MOE_TOPK], moe_w1[0], moe_w3[0], moe_w2[0])

    x2, q, k, v = _odd_in(x1, ybuf, norm_mix[1], od_w_in[0])
    layer_idx = 1
    lam_init = 0.8 - 0.6 * math.exp(-0.3 * layer_idx)
    lam = (jnp.exp(jnp.sum(od_lam_q1[0].astype(F32) * od_lam_k1[0].astype(F32)))
           - jnp.exp(jnp.sum(od_lam_q2[0].astype(F32) * od_lam_k2[0].astype(F32))) + lam_init)
    o = _diff_attention(q, k, v, lam, lam_init, od_subln[0], rel_bias, batch, seq)
    x3, h, ids, gates = _out_router(o, False, x2, od_w_out[0], norm_ffn[1],
                                    moe_w_group[1], moe_b_group[1], moe_w_expert[1], moe_b_expert[1])
    ybuf = _moe_ffn(h, ids[:, :MOE_TOPK], gates[:, :MOE_TOPK], moe_w1[1], moe_w3[1], moe_w2[1])
    out = _final(x3, ybuf, norm_final)
    return out.reshape(batch, seq, d)
```

```python
import functools
import math

import jax
import jax.numpy as jnp
from jax import lax
from jax.experimental import pallas as pl
from jax.experimental.pallas import tpu as pltpu

F32 = jnp.float32
BF16 = jnp.bfloat16

GRID_W = 64
HEAD_DIM = 64
EPS = 1e-6
A_HEADS = 8
A_KV_HEADS = 2
A_GROUP = A_HEADS // A_KV_HEADS
ROPE_THETA = 10000.0
ROPE_HALF = HEAD_DIM // 4
B_GROUP_WIDTH = 64
CONV_WIDTH = 31
CONV_HALO = 16
C_HEADS = 8
C_HEAD_DIM = 64
REL_BUCKETS = 32
REL_MAX_DIST = 128
MOE_GROUPS = 4
MOE_PER_GROUP = 8
MOE_EXPERTS = MOE_GROUPS * MOE_PER_GROUP
MOE_TOPK = 2
LOG2E = 1.4426950408889634
NEG_BIG = -1e30
LANES = 128
SUBLANES = 8
MOE_DMA_UNROLL = 8

TM_PROJ = 512
TQ_A = 512
TK_A = 512
HEADS_PER_STEP_A = 2
TQ_C = 512
TK_C = 512
TS_CONV = 256
CONV_SUB = 64
MOE_BM = 128
VMEM_LIMIT = 56 * 1024 * 1024


def _cparams(*sem):
    return pltpu.CompilerParams(dimension_semantics=sem, vmem_limit_bytes=VMEM_LIMIT)


def _split_dot(a, b_bf16):
    hi = a.astype(BF16)
    lo = (a - hi.astype(F32)).astype(BF16)
    return (jnp.dot(hi, b_bf16, preferred_element_type=F32)
            + jnp.dot(lo, b_bf16, preferred_element_type=F32))


def _rms_rows(x, g, eps):
    ms = jnp.mean(x * x, axis=-1, keepdims=True)
    return x * lax.rsqrt(ms + eps) * g


def _sigmoid(x):
    return 1.0 / (1.0 + jnp.exp(-x))


def _rows_from_tiles(ref):
    rows = ref.shape[0] // SUBLANES
    return jnp.concatenate([ref[pl.ds(s, rows, stride=SUBLANES), :] for s in range(SUBLANES)], axis=1)


def _rows_to_tiles(ref, val):
    rows = val.shape[0]
    for s in range(SUBLANES):
        ref[pl.ds(s, rows, stride=SUBLANES), :] = val[:, s * LANES:(s + 1) * LANES]


def _even_in_kernel(x_ref, g_ref, w_ref, gqk_ref, bd_ref, cos_ref, s1_ref, s2_ref,
                    q_ref, k_ref, v_ref, glu_ref, *, qk_w, q_w, kv_w, b_w):
    h = _rms_rows(x_ref[...], g_ref[...], EPS)
    z = jnp.dot(h.astype(BF16), w_ref[...], preferred_element_type=F32)
    qk = z[:, :qk_w]
    ms = _split_dot(qk * qk, bd_ref[...])
    qkn = qk * lax.rsqrt(ms + EPS) * gqk_ref[...]
    cos, s1, s2 = cos_ref[...], s1_ref[...], s2_ref[...]
    parts = []
    for c in range(qk_w // LANES):
        blk = qkn[:, c * LANES:(c + 1) * LANES]
        up = pltpu.roll(blk, LANES - ROPE_HALF, 1)
        dn = pltpu.roll(blk, ROPE_HALF, 1)
        parts.append(blk * cos + up * s1 + dn * s2)
    rot = jnp.concatenate(parts, axis=1)
    q_ref[...] = rot[:, :q_w].astype(BF16)
    k_ref[...] = rot[:, q_w:qk_w].astype(BF16)
    v_ref[...] = z[:, qk_w:qk_w + kv_w].astype(BF16)
    a = z[:, qk_w + kv_w:qk_w + kv_w + b_w]
    gate = z[:, qk_w + kv_w + b_w:qk_w + kv_w + 2 * b_w]
    glu_ref[...] = a * _sigmoid(gate)


def _even_in(x2d, g, w_in, q_norm, k_norm, seq):
    n, d = x2d.shape
    q_w, kv_w, b_w = A_HEADS * HEAD_DIM, A_KV_HEADS * HEAD_DIM, d // 2
    qk_w = q_w + kv_w
    tm = TM_PROJ
    gq = jnp.tile(q_norm.astype(F32), A_HEADS) * (HEAD_DIM ** -0.5 * LOG2E)
    gk = jnp.tile(k_norm.astype(F32), A_KV_HEADS)
    gqk = jnp.concatenate([gq, gk])[None, :]
    hid = jnp.arange(qk_w) // HEAD_DIM
    bd = (hid[:, None] == hid[None, :]).astype(BF16) * (1.0 / HEAD_DIM)
    pos = jnp.arange(seq, dtype=jnp.int32)
    row, col = pos // GRID_W, pos % GRID_W
    m = ROPE_HALF
    inv = ROPE_THETA ** (-jnp.arange(m, dtype=F32) / m)
    ang_r = row.astype(F32)[:, None] * inv[None, :]
    ang_c = col.astype(F32)[:, None] * inv[None, :]
    zero = jnp.zeros_like(ang_r)
    cos64 = jnp.concatenate([jnp.cos(ang_r), jnp.cos(ang_r), jnp.cos(ang_c), jnp.cos(ang_c)], axis=1)
    s1_64 = jnp.concatenate([-jnp.sin(ang_r), zero, -jnp.sin(ang_c), zero], axis=1)
    s2_64 = jnp.concatenate([zero, jnp.sin(ang_r), zero, jnp.sin(ang_c)], axis=1)
    rep = LANES // HEAD_DIM
    cos_t, s1_t, s2_t = (jnp.tile(t, (1, rep)) for t in (cos64, s1_64, s2_64))
    nseq = seq // tm
    tab_spec = pl.BlockSpec((tm, LANES), lambda i: (i % nseq, 0))
    full = lambda shape: pl.BlockSpec(shape, lambda i: (0,) * len(shape))
    kern = functools.partial(_even_in_kernel, qk_w=qk_w, q_w=q_w, kv_w=kv_w, b_w=b_w)
    return pl.pallas_call(
        kern,
        grid=(n // tm,),
        in_specs=[pl.BlockSpec((tm, d), lambda i: (i, 0)), full((1, d)), full(w_in.shape),
                  full((1, qk_w)), full((qk_w, qk_w)), tab_spec, tab_spec, tab_spec],
        out_specs=[pl.BlockSpec((tm, q_w), lambda i: (i, 0)), pl.BlockSpec((tm, kv_w), lambda i: (i, 0)),
                   pl.BlockSpec((tm, kv_w), lambda i: (i, 0)), pl.BlockSpec((tm, b_w), lambda i: (i, 0))],
        out_shape=[jax.ShapeDtypeStruct((n, q_w), BF16), jax.ShapeDtypeStruct((n, kv_w), BF16),
                   jax.ShapeDtypeStruct((n, kv_w), BF16), jax.ShapeDtypeStruct((n, b_w), F32)],
        compiler_params=_cparams("parallel"),
        name="even_in_proj",
    )(x2d, g[None, :].astype(F32), w_in.astype(BF16), gqk, bd, cos_t, s1_t, s2_t)


def _flash_pipelined(nk, n_maps, score_fn, pv_fn, s_a, s_b, tq, dv):
    assert nk % 2 == 0 and nk >= 2

    def scores(j, s_ref):
        cms = []
        for c in range(n_maps):
            s = score_fn(j, c)
            s_ref[c] = s
            cms.append(jnp.max(s, axis=0, keepdims=True))
        return tuple(cms)

    def softmax(j, s_ref, cms, state):
        new = []
        for c in range(n_maps):
            m, l, acc = state[c]
            m_new = jnp.maximum(m, cms[c])
            alpha = jnp.exp2(m - m_new)
            p = jnp.exp2(s_ref[c] - m_new)
            l = alpha * l + jnp.sum(p, axis=0, keepdims=True)
            acc = alpha * acc + pv_fn(j, c, p.astype(BF16))
            new.append((m_new, l, acc))
        return tuple(new)

    init = tuple((jnp.full((1, tq), NEG_BIG, F32), jnp.zeros((1, tq), F32), jnp.zeros((dv, tq), F32))
                 for _ in range(n_maps))
    cms0 = scores(0, s_a)

    def pair(t, carry):
        cms_a, state = carry
        j = 2 * t
        cms_b = scores(j + 1, s_b)
        state = softmax(j, s_a, cms_a, state)
        cms_a = scores(j + 2, s_a)
        state = softmax(j + 1, s_b, cms_b, state)
        return cms_a, state

    cms_a, state = lax.fori_loop(0, nk // 2 - 1, pair, (cms0, init))
    cms_b = scores(nk - 1, s_b)
    state = softmax(nk - 2, s_a, cms_a, state)
    state = softmax(nk - 1, s_b, cms_b, state)
    return tuple((l, acc) for _, l, acc in state)


def _gqa_kernel(qT_ref, k_ref, vT_ref, o_ref, s_a, s_b, *, nk):
    nh, hd, tq = qT_ref.shape[1:]
    qs = [qT_ref[0, h] for h in range(nh)]
    score_fn = lambda j, c: jnp.dot(k_ref[0, 0, j], qs[c], preferred_element_type=F32)
    pv_fn = lambda j, c, p: jnp.dot(vT_ref[0, 0, j], p, preferred_element_type=F32)
    res = _flash_pipelined(nk, nh, score_fn, pv_fn, s_a, s_b, tq, hd)
    for h, (l, acc) in enumerate(res):
        o_ref[0, h] = (acc / l).astype(o_ref.dtype)


def _gqa_attention(q, k, v, batch, seq):
    hd, tq, tk, nh = HEAD_DIM, TQ_A, TK_A, HEADS_PER_STEP_A
    nk = seq // tk
    hsteps = A_HEADS // nh
    per_kv = A_GROUP // nh
    qT = q.reshape(batch, seq, A_HEADS, hd).transpose(0, 2, 3, 1)
    kc = k.reshape(batch, nk, tk, A_KV_HEADS, hd).transpose(0, 3, 1, 2, 4)
    vT = v.reshape(batch, nk, tk, A_KV_HEADS, hd).transpose(0, 3, 1, 4, 2)
    oT = pl.pallas_call(
        functools.partial(_gqa_kernel, nk=nk),
        grid=(batch, hsteps, seq // tq),
        in_specs=[pl.BlockSpec((1, nh, hd, tq), lambda b, h, i: (b, h, 0, i)),
                  pl.BlockSpec((1, 1, nk, tk, hd), lambda b, h, i: (b, h // per_kv, 0, 0, 0)),
                  pl.BlockSpec((1, 1, nk, hd, tk), lambda b, h, i: (b, h // per_kv, 0, 0, 0))],
        out_specs=pl.BlockSpec((1, nh, hd, tq), lambda b, h, i: (b, h, 0, i)),
        out_shape=jax.ShapeDtypeStruct((batch, A_HEADS, hd, seq), BF16),
        scratch_shapes=[pltpu.VMEM((nh, tk, tq), F32), pltpu.VMEM((nh, tk, tq), F32)],
        compiler_params=_cparams("parallel", "parallel", "parallel"),
        name="gqa_attention",
    )(qT, kc, vT)
    return oT.transpose(0, 3, 1, 2).reshape(batch * seq, A_HEADS * hd)


def _conv_kernel(prev_ref, cur_ref, next_ref, w_ref, b_ref, g_ref, beta_ref, bd_ref, o_ref, xe_ref, *, ts):
    i = pl.program_id(1)
    last = pl.num_programs(1) - 1
    halo = CONV_HALO
    xe_ref[0:halo, :] = jnp.where(i > 0, prev_ref[0], 0.0)
    xe_ref[halo:halo + ts, :] = cur_ref[0]
    xe_ref[halo + ts:halo + ts + halo, :] = jnp.where(i < last, next_ref[0], 0.0)
    w = w_ref[...]
    first = halo - CONV_WIDTH // 2
    for r0 in range(0, ts, CONV_SUB):
        acc = jnp.zeros((CONV_SUB, w.shape[1]), F32) + b_ref[...]
        for j in range(CONV_WIDTH):
            acc = acc + w[j:j + 1, :] * xe_ref[r0 + first + j:r0 + first + j + CONV_SUB, :]
        mu = _split_dot(acc, bd_ref[...])
        cen = acc - mu
        var = _split_dot(cen * cen, bd_ref[...])
        hn = cen * lax.rsqrt(var + 1e-5) * g_ref[...] + beta_ref[...]
        o_ref[0, r0:r0 + CONV_SUB, :] = (hn * _sigmoid(hn)).astype(o_ref.dtype)


def _conformer_conv(glu, conv_w, conv_b, gn_g, gn_b, batch, seq):
    c = glu.shape[1]
    ts, halo = TS_CONV, CONV_HALO
    x3 = glu.reshape(batch, seq, c)
    hpb = ts // halo
    nh = seq // halo
    gid = jnp.arange(c) // B_GROUP_WIDTH
    bd = (gid[:, None] == gid[None, :]).astype(BF16) * (1.0 / B_GROUP_WIDTH)
    wpad = jnp.zeros((32, c), F32).at[:CONV_WIDTH].set(conv_w.astype(F32))
    vec = lambda a: a[None, :].astype(F32)
    full = lambda shape: pl.BlockSpec(shape, lambda b, i: (0,) * len(shape))
    out = pl.pallas_call(
        functools.partial(_conv_kernel, ts=ts),
        grid=(batch, seq // ts),
        in_specs=[pl.BlockSpec((1, halo, c), lambda b, i: (b, jnp.maximum(i * hpb - 1, 0), 0)),
                  pl.BlockSpec((1, ts, c), lambda b, i: (b, i, 0)),
                  pl.BlockSpec((1, halo, c), lambda b, i: (b, jnp.minimum((i + 1) * hpb, nh - 1), 0)),
                  full((32, c)), full((1, c)), full((1, c)), full((1, c)), full((c, c))],
        out_specs=pl.BlockSpec((1, ts, c), lambda b, i: (b, i, 0)),
        out_shape=jax.ShapeDtypeStruct((batch, seq, c), BF16),
        scratch_shapes=[pltpu.VMEM((ts + 2 * halo, c), F32)],
        compiler_params=_cparams("parallel", "parallel"),
        name="conformer_conv",
    )(x3, x3, x3, wpad, vec(conv_b), vec(gn_g), vec(gn_b), bd)
    return out.reshape(batch * seq, c)


def _route(logits):
    lane = lax.broadcasted_iota(jnp.int32, logits.shape, 1)
    big = jnp.int32(1 << 20)
    ninf = jnp.float32(-jnp.inf)
    gmask = (lane >= MOE_EXPERTS) & (lane < MOE_EXPERTS + MOE_GROUPS)
    lg = jnp.where(gmask, logits, ninf)
    mg = jnp.max(lg, axis=-1, keepdims=True)
    sg = jnp.sum(jnp.exp(lg - mg), axis=-1, keepdims=True)
    g_val = 1.0 / sg
    g_idx = jnp.min(jnp.where(lg == mg, lane - MOE_EXPERTS, big), axis=-1, keepdims=True)
    emask = (lane < MOE_EXPERTS) & ((lane // MOE_PER_GROUP) == g_idx)
    le = jnp.where(emask, logits, ninf)
    m1 = jnp.max(le, axis=-1, keepdims=True)
    se = jnp.sum(jnp.exp(le - m1), axis=-1, keepdims=True)
    i1 = jnp.min(jnp.where(le == m1, lane, big), axis=-1, keepdims=True)
    le2 = jnp.where(lane == i1, ninf, le)
    m2 = jnp.max(le2, axis=-1, keepdims=True)
    i2 = jnp.min(jnp.where(le2 == m2, lane, big), axis=-1, keepdims=True)
    e1 = 1.0 / se
    e2 = jnp.exp(m2 - m1) / se
    norm = g_val / (e1 + e2)
    ids = jnp.where(lane == 0, i1, jnp.where(lane == 1, i2, 0))
    gates = jnp.where(lane == 0, e1 * norm, jnp.where(lane == 1, e2 * norm, 0.0))
    return ids, gates


def _out_router_kernel(ya_ref, yb_ref, x_ref, wa_ref, wb_ref, g_ref, wrh_ref, wrl_ref, br_ref,
                       x1_ref, h_ref, ids_ref, gates_ref):
    x1 = (x_ref[...] + jnp.dot(ya_ref[...], wa_ref[...], preferred_element_type=F32)
          + jnp.dot(yb_ref[...], wb_ref[...], preferred_element_type=F32))
    x1_ref[...] = x1
    h = _rms_rows(x1, g_ref[...], EPS)
    _rows_to_tiles(h_ref, h)
    hi = h.astype(BF16)
    lo = (h - hi.astype(F32)).astype(BF16)
    logits = (jnp.dot(hi, wrh_ref[...], preferred_element_type=F32)
              + jnp.dot(hi, wrl_ref[...], preferred_element_type=F32)
              + jnp.dot(lo, wrh_ref[...], preferred_element_type=F32) + br_ref[...])
    ids, gates = _route(logits)
    ids_ref[...] = ids
    gates_ref[...] = gates


def _out_router(y, split, x2d, w_out, g_ffn, w_group, b_group, w_expert, b_expert):
    n, d = x2d.shape
    tm = TM_PROJ
    half = w_out.shape[0] // 2
    if split:
        ya, yb = y
        ya_spec = pl.BlockSpec((tm, half), lambda i: (i, 0))
        yb_spec = pl.BlockSpec((tm, half), lambda i: (i, 0))
    else:
        ya = yb = y
        ya_spec = pl.BlockSpec((tm, half), lambda i: (i, 0))
        yb_spec = pl.BlockSpec((tm, half), lambda i: (i, 1))
    wb16 = w_out.astype(BF16)
    wr = jnp.zeros((d, LANES), F32).at[:, :MOE_EXPERTS].set(w_expert.astype(F32))
    wr = wr.at[:, MOE_EXPERTS:MOE_EXPERTS + MOE_GROUPS].set(w_group.astype(F32))
    wrh = wr.astype(BF16)
    wrl = (wr - wrh.astype(F32)).astype(BF16)
    br = jnp.zeros((1, LANES), F32).at[0, :MOE_EXPERTS].set(b_expert.astype(F32))
    br = br.at[0, MOE_EXPERTS:MOE_EXPERTS + MOE_GROUPS].set(b_group.astype(F32))
    full = lambda shape: pl.BlockSpec(shape, lambda i: (0,) * len(shape))
    row = lambda w: pl.BlockSpec((tm, w), lambda i: (i, 0))
    return pl.pallas_call(
        _out_router_kernel,
        grid=(n // tm,),
        in_specs=[ya_spec, yb_spec, row(d), full((half, d)), full((half, d)), full((1, d)),
                  full((d, LANES)), full((d, LANES)), full((1, LANES))],
        out_specs=[row(d), pl.BlockSpec((tm * SUBLANES, LANES), lambda i: (i, 0)), row(LANES), row(LANES)],
        out_shape=[jax.ShapeDtypeStruct((n, d), F32), jax.ShapeDtypeStruct((n * SUBLANES, LANES), F32),
                   jax.ShapeDtypeStruct((n, LANES), jnp.int32), jax.ShapeDtypeStruct((n, LANES), F32)],
        compiler_params=_cparams("parallel"),
        name="out_proj_router",
    )(ya, yb, x2d, wb16[:half], wb16[half:], g_ffn[None, :].astype(F32), wrh, wrl, br)


def _moe_dispatch(ids, gates, n):
    bm = MOE_BM
    a = n * MOE_TOPK
    p = a + MOE_EXPERTS * bm
    flat_e = ids.reshape(-1)
    flat_w = gates.reshape(-1)
    order = jnp.argsort(flat_e).astype(jnp.int32)
    experts = jnp.arange(MOE_EXPERTS, dtype=jnp.int32)
    counts = jnp.sum((flat_e[:, None] == experts[None, :]).astype(jnp.int32), axis=0)
    pcounts = (counts + bm - 1) // bm * bm
    starts = jnp.cumsum(counts) - counts
    pends = jnp.cumsum(pcounts)
    pstarts = pends - pcounts
    n_blk = p // bm
    blk_start = jnp.arange(n_blk, dtype=jnp.int32) * bm
    blk_e = jnp.minimum(jnp.sum((blk_start[:, None] >= pends[None, :]).astype(jnp.int32), axis=1),
                        MOE_EXPERTS - 1)
    off = (blk_start - pstarts[blk_e])[:, None] + jnp.arange(bm, dtype=jnp.int32)[None, :]
    valid = (off < counts[blk_e][:, None]).reshape(p)
    sidx = jnp.clip(starts[blk_e][:, None] + off, 0, a - 1).reshape(p)
    assign = order[sidx]
    tok = assign // MOE_TOPK
    slot = assign % MOE_TOPK
    src_tok = jnp.where(valid, tok, 0)
    dump = a + jnp.cumsum(1 - valid.astype(jnp.int32)) - 1
    dst_row = jnp.where(valid, slot * n + tok, dump).astype(jnp.int32)
    row_w = jnp.where(valid, flat_w[assign], 0.0)
    tok_bits = (n - 1).bit_length()
    assert tok_bits + (p - 1).bit_length() <= 32
    packed = lax.bitcast_convert_type(
        src_tok.astype(jnp.uint32) | (dst_row.astype(jnp.uint32) << tok_bits), jnp.int32)
    return blk_e, packed, row_w[:, None], tok_bits


def _moe_kernel(be_ref, idx_ref, h_hbm, w1_ref, w3_ref, w2_ref, rw_ref, y_hbm,
                xbuf, obuf, gsem, ssem, *, tok_bits):
    del be_ref
    bm = MOE_BM
    i = pl.program_id(0)
    nb = pl.num_programs(0)
    slot = i % 2
    tok_mask = (1 << tok_bits) - 1

    tile = lambda r: pl.ds(pl.multiple_of(r * SUBLANES, SUBLANES), SUBLANES)

    def row_loop(body):
        lax.fori_loop(0, bm, lambda r, c: (body(r), c)[1], 0, unroll=MOE_DMA_UNROLL)

    def gather_start(blk, sl):
        def body(r):
            t = idx_ref[blk * bm + r] & tok_mask
            pltpu.make_async_copy(h_hbm.at[tile(t)], xbuf.at[sl, tile(r)], gsem.at[sl]).start()
        row_loop(body)

    def gather_wait(sl):
        pltpu.make_async_copy(h_hbm.at[pl.ds(0, bm * SUBLANES)], xbuf.at[sl], gsem.at[sl]).wait()

    def scatter_start(blk, sl):
        def body(r):
            t = lax.shift_right_logical(idx_ref[blk * bm + r], tok_bits)
            pltpu.make_async_copy(obuf.at[sl, tile(r)], y_hbm.at[tile(t)], ssem.at[sl]).start()
        row_loop(body)

    def scatter_wait(sl):
        pltpu.make_async_copy(obuf.at[sl], y_hbm.at[pl.ds(0, bm * SUBLANES)], ssem.at[sl]).wait()

    @pl.when(i == 0)
    def _():
        gather_start(0, 0)

    @pl.when(i + 1 < nb)
    def _():
        gather_start(i + 1, 1 - slot)

    gather_wait(slot)
    x = _rows_from_tiles(xbuf.at[slot]).astype(BF16)
    h1 = jnp.dot(x, w1_ref[0].astype(BF16), preferred_element_type=F32)
    h3 = jnp.dot(x, w3_ref[0].astype(BF16), preferred_element_type=F32)
    hm = (h1 * _sigmoid(h1) * h3).astype(BF16)
    y = jnp.dot(hm, w2_ref[0].astype(BF16), preferred_element_type=F32) * rw_ref[...]

    @pl.when(i >= 2)
    def _():
        scatter_wait(slot)

    _rows_to_tiles(obuf.at[slot], y)
    scatter_start(i, slot)

    @pl.when(i == nb - 1)
    def _():
        scatter_wait(slot)
        scatter_wait(1 - slot)


def _moe_ffn(h_tiles, ids, gates, w1, w3, w2):
    d = SUBLANES * LANES
    n = h_tiles.shape[0] // SUBLANES
    assert w1.shape[1] == d and h_tiles.shape[1] == LANES
    bm = MOE_BM
    hid = w1.shape[-1]
    blk_e, packed, row_w, tok_bits = _moe_dispatch(ids, gates, n)
    n_blk = blk_e.shape[0]
    rows_out = n * MOE_TOPK + MOE_EXPERTS * bm
    grid_spec = pltpu.PrefetchScalarGridSpec(
        num_scalar_prefetch=2,
        grid=(n_blk,),
        in_specs=[pl.BlockSpec(memory_space=pl.ANY),
                  pl.BlockSpec((1, d, hid), lambda i, be, ix: (be[i], 0, 0)),
                  pl.BlockSpec((1, d, hid), lambda i, be, ix: (be[i], 0, 0)),
                  pl.BlockSpec((1, hid, d), lambda i, be, ix: (be[i], 0, 0)),
                  pl.BlockSpec((bm, 1), lambda i, be, ix: (i, 0))],
        out_specs=pl.BlockSpec(memory_space=pl.ANY),
        scratch_shapes=[pltpu.VMEM((2, bm * SUBLANES, LANES), F32), pltpu.VMEM((2, bm * SUBLANES, LANES), F32),
                        pltpu.SemaphoreType.DMA((2,)), pltpu.SemaphoreType.DMA((2,))],
    )
    return pl.pallas_call(
        functools.partial(_moe_kernel, tok_bits=tok_bits),
        grid_spec=grid_spec,
        out_shape=jax.ShapeDtypeStruct((rows_out * SUBLANES, LANES), F32),
        compiler_params=_cparams("arbitrary"),
        name="moe_ffn",
    )(blk_e, packed, h_tiles, w1, w3, w2, row_w)


def _odd_in_kernel(x_ref, y0_ref, y1_ref, g_ref, w_ref, xo_ref, q_ref, k_ref, v_ref, *, cw):
    x = x_ref[...] + _rows_from_tiles(y0_ref) + _rows_from_tiles(y1_ref)
    xo_ref[...] = x
    h = _rms_rows(x, g_ref[...], EPS)
    z = jnp.dot(h.astype(BF16), w_ref[...], preferred_element_type=F32)
    q_ref[...] = (z[:, :cw] * (C_HEAD_DIM ** -0.5 * LOG2E)).astype(BF16)
    k_ref[...] = z[:, cw:2 * cw].astype(BF16)
    v_ref[...] = z[:, 2 * cw:3 * cw].astype(BF16)


def _odd_in(x2d, ybuf, g, w_in):
    n, d = x2d.shape
    tm = TM_PROJ
    cw = w_in.shape[1] // 3
    nt = n // tm
    row = lambda w: pl.BlockSpec((tm, w), lambda i: (i, 0))
    full = lambda shape: pl.BlockSpec(shape, lambda i: (0,) * len(shape))
    return pl.pallas_call(
        functools.partial(_odd_in_kernel, cw=cw),
        grid=(nt,),
        in_specs=[row(d), pl.BlockSpec((tm * SUBLANES, LANES), lambda i: (i, 0)),
                  pl.BlockSpec((tm * SUBLANES, LANES), lambda i: (i + nt, 0)), full((1, d)), full(w_in.shape)],
        out_specs=[row(d), row(cw), row(cw), row(cw)],
        out_shape=[jax.ShapeDtypeStruct((n, d), F32)] + [jax.ShapeDtypeStruct((n, cw), BF16)] * 3,
        compiler_params=_cparams("parallel"),
        name="odd_in_proj",
    )(x2d, ybuf, ybuf, g[None, :].astype(F32), w_in.astype(BF16))


def _t5_bucket(rel):
    nb = REL_BUCKETS // 2
    max_exact = nb // 2
    ret = jnp.where(rel > 0, nb, 0).astype(jnp.int32)
    n = jnp.abs(rel)
    nf = jnp.maximum(n, 1).astype(F32)
    large = max_exact + (jnp.log(nf / max_exact) / math.log(REL_MAX_DIST / max_exact) * (nb - max_exact)).astype(jnp.int32)
    large = jnp.minimum(large, nb - 1)
    return ret + jnp.where(n < max_exact, n, large)


def _diff_kernel(lam_ref, qT_ref, k_ref, vT_ref, band_ref, gsub_ref, o_ref, s_a, s_b, *, nk):
    i = pl.program_id(2)
    hd, tq = qT_ref.shape[3:]
    dv = vT_ref.shape[3]
    qs = [qT_ref[0, 0, c] for c in range(2)]

    def score_fn(j, c):
        bias = band_ref[0, jnp.clip(j - i, -2, 2) + 2]
        return jnp.dot(k_ref[0, 0, c, j], qs[c], preferred_element_type=F32) + bias

    pv_fn = lambda j, c, p: jnp.dot(vT_ref[0, 0, j], p, preferred_element_type=F32)
    (l0, a0), (l1, a1) = _flash_pipelined(nk, 2, score_fn, pv_fn, s_a, s_b, tq, dv)
    o = a0 / l0 - lam_ref[...] * (a1 / l1)
    ms = jnp.mean(o * o, axis=0, keepdims=True)
    o_ref[0, 0] = (o * lax.rsqrt(ms + 1e-5) * gsub_ref[...]).astype(o_ref.dtype)


def _diff_attention(q, k, v, lam, lam_init, subln, rel_bias, batch, seq):
    hd, tq, tk = C_HEAD_DIM, TQ_C, TK_C
    dv = 2 * hd
    nk = seq // tk
    qT = q.reshape(batch, seq, C_HEADS, 2, hd).transpose(0, 2, 3, 4, 1)
    kc = k.reshape(batch, nk, tk, C_HEADS, 2, hd).transpose(0, 3, 4, 1, 2, 5)
    vT = v.reshape(batch, nk, tk, C_HEADS, dv).transpose(0, 3, 1, 4, 2)
    assert tk == tq and tk >= REL_MAX_DIST
    top = 3 * tk - 1
    rev = rel_bias.astype(F32)[_t5_bucket(top - jnp.arange(2 * top + 1, dtype=jnp.int32))].T * LOG2E
    first = (top - jnp.arange(-2, 3, dtype=jnp.int32)[:, None] * tk
             - jnp.arange(tk, dtype=jnp.int32)[None, :])
    window = lambda vec, st: lax.dynamic_slice(vec, (st,), (tq,))
    band = jax.vmap(lambda vec: jax.vmap(jax.vmap(functools.partial(window, vec)))(first))(rev)
    lam_vec = jnp.full((1, tq), lam, F32)
    gsub = (subln.astype(F32) * (1.0 - lam_init))[:, None]
    oT = pl.pallas_call(
        functools.partial(_diff_kernel, nk=nk),
        grid=(batch, C_HEADS, seq // tq),
        in_specs=[pl.BlockSpec((1, tq), lambda b, h, i: (0, 0)),
                  pl.BlockSpec((1, 1, 2, hd, tq), lambda b, h, i: (b, h, 0, 0, i)),
                  pl.BlockSpec((1, 1, 2, nk, tk, hd), lambda b, h, i: (b, h, 0, 0, 0, 0)),
                  pl.BlockSpec((1, 1, nk, dv, tk), lambda b, h, i: (b, h, 0, 0, 0)),
                  pl.BlockSpec((1, 5, tk, tq), lambda b, h, i: (h, 0, 0, 0)),
                  pl.BlockSpec((dv, 1), lambda b, h, i: (0, 0))],
        out_specs=pl.BlockSpec((1, 1, dv, tq), lambda b, h, i: (b, h, 0, i)),
        out_shape=jax.ShapeDtypeStruct((batch, C_HEADS, dv, seq), BF16),
        scratch_shapes=[pltpu.VMEM((2, tk, tq), F32), pltpu.VMEM((2, tk, tq), F32)],
        compiler_params=_cparams("parallel", "parallel", "parallel"),
        name="diff_attention",
    )(lam_vec, qT, kc, vT, band, gsub)
    return oT.transpose(0, 3, 1, 2).reshape(batch * seq, C_HEADS * dv)


def _final_kernel(x_ref, y0_ref, y1_ref, g_ref, o_ref):
    x = x_ref[...] + _rows_from_tiles(y0_ref) + _rows_from_tiles(y1_ref)
    o_ref[...] = _rms_rows(x, g_ref[...], EPS)


def _final(x2d, ybuf, g):
    n, d = x2d.shape
    tm = TM_PROJ
    nt = n // tm
    row = pl.BlockSpec((tm, d), lambda i: (i, 0))
    return pl.pallas_call(
        _final_kernel,
        grid=(nt,),
        in_specs=[row, pl.BlockSpec((tm * SUBLANES, LANES), lambda i: (i, 0)),
                  pl.BlockSpec((tm * SUBLANES, LANES), lambda i: (i + nt, 0)), pl.BlockSpec((1, d), lambda i: (0, 0))],
        out_specs=row,
        out_shape=jax.ShapeDtypeStruct((n, d), F32),
        compiler_params=_cparams("parallel"),
        name="final_norm",
    )(x2d, ybuf, ybuf, g[None, :].astype(F32))


def kernel(x, norm_mix, norm_ffn, norm_final, ev_w_in, ev_q_norm, ev_k_norm, ev_conv_w, ev_conv_b, ev_gn_g, ev_gn_b, ev_w_out, od_w_in, od_lam_q1, od_lam_k1, od_lam_q2, od_lam_k2, od_subln, od_w_out, rel_bias, moe_w_group, moe_b_group, moe_w_expert, moe_b_expert, moe_w1, moe_w3, moe_w2):
    batch, seq, d = x.shape
    n = batch * seq
    x2d = x.reshape(n, d).astype(F32)

    q, k, v, glu = _even_in(x2d, norm_mix[0], ev_w_in[0], ev_q_norm[0], ev_k_norm[0], seq)
    ya = _gqa_attention(q, k, v, batch, seq)
    yb = _conformer_conv(glu, ev_conv_w[0], ev_conv_b[0], ev_gn_g[0], ev_gn_b[0], batch, seq)
    x1, h, ids, gates = _out_router((ya, yb), True, x2d, ev_w_out[0], norm_ffn[0],
                                    moe_w_group[0], moe_b_group[0], moe_w_expert[0], moe_b_expert[0])
    ybuf = _moe_ffn(h, ids[:, :MOE_TOPK], gates[:, :MOE_TOPK], moe_w1[0], moe_w3[0], moe_w2[0])

    x2, q, k, v = _odd_in(x1, ybuf, norm_mix[1], od_w_in[0])
    layer_idx = 1
    lam_init = 0.8 - 0.6 * math.exp(-0.3 * layer_idx)
    lam = (jnp.exp(jnp.sum(od_lam_q1[0].astype(F32) * od_lam_k1[0].astype(F32)))
           - jnp.exp(jnp.sum(od_lam_q2[0].astype(F32) * od_lam_k2[0].astype(F32))) + lam_init)
    o = _diff_attention(q, k, v, lam, lam_init, od_subln[0], rel_bias, batch, seq)
    x3, h, ids, gates = _out_router(o, False, x2, od_w_out[0], norm_ffn[1],
                                    moe_w_group[1], moe_b_group[1], moe_w_expert[1], moe_b_expert[1])
    ybuf = _moe_ffn(h, ids[:, :MOE_TOPK], gates[:, :MOE_TOPK], moe_w1[1], moe_w3[1], moe_w2[1])
    out = _final(x3, ybuf, norm_final)
    return out.reshape(batch, seq, d)
```

```python
import functools
import math

import jax
import jax.numpy as jnp
from jax import lax
from jax.experimental import pallas as pl
from jax.experimental.pallas import tpu as pltpu

F32 = jnp.float32
BF16 = jnp.bfloat16

GRID_W = 64
HEAD_DIM = 64
EPS = 1e-6
A_HEADS = 8
A_KV_HEADS = 2
A_GROUP = A_HEADS // A_KV_HEADS
ROPE_THETA = 10000.0
ROPE_HALF = HEAD_DIM // 4
B_GROUP_WIDTH = 64
CONV_WIDTH = 31
CONV_HALO = 16
C_HEADS = 8
C_HEAD_DIM = 64
REL_BUCKETS = 32
REL_MAX_DIST = 128
MOE_GROUPS = 4
MOE_PER_GROUP = 8
MOE_EXPERTS = MOE_GROUPS * MOE_PER_GROUP
MOE_TOPK = 2
LOG2E = 1.4426950408889634
NEG_BIG = -1e30
LANES = 128
SUBLANES = 8
MOE_DMA_UNROLL = 8

TM_PROJ = 512
HEADS_PER_STEP_A = 2
TS_CONV = 256
CONV_SUB = 64
MOE_BM = 128
VMEM_LIMIT = 56 * 1024 * 1024


def _cparams(*sem):
    return pltpu.CompilerParams(dimension_semantics=sem, vmem_limit_bytes=VMEM_LIMIT)


def _split_dot(a, b_bf16):
    hi = a.astype(BF16)
    lo = (a - hi.astype(F32)).astype(BF16)
    return (jnp.dot(hi, b_bf16, preferred_element_type=F32)
            + jnp.dot(lo, b_bf16, preferred_element_type=F32))


def _rms_rows(x, g, eps):
    ms = jnp.mean(x * x, axis=-1, keepdims=True)
    return x * lax.rsqrt(ms + eps) * g


def _sigmoid(x):
    return 1.0 / (1.0 + jnp.exp(-x))


def _chunkT_spec(width, tm, nseq):
    return pl.BlockSpec((1, 1, width, tm), lambda i: (i // nseq, i % nseq, 0, 0))


def _rows_from_tiles(ref):
    rows = ref.shape[0] // SUBLANES
    return jnp.concatenate([ref[pl.ds(s, rows, stride=SUBLANES), :] for s in range(SUBLANES)], axis=1)


def _rows_to_tiles(ref, val):
    rows = val.shape[0]
    for s in range(SUBLANES):
        ref[pl.ds(s, rows, stride=SUBLANES), :] = val[:, s * LANES:(s + 1) * LANES]


def _even_in_kernel(x_ref, g_ref, w_ref, gqk_ref, bd_ref, cos_ref, s1_ref, s2_ref,
                    q_ref, k_ref, v_ref, glu_ref, *, qk_w, q_w, kv_w, b_w):
    h = _rms_rows(x_ref[...], g_ref[...], EPS)
    z = jnp.dot(h.astype(BF16), w_ref[...], preferred_element_type=F32)
    qk = z[:, :qk_w]
    ms = _split_dot(qk * qk, bd_ref[...])
    qkn = qk * lax.rsqrt(ms + EPS) * gqk_ref[...]
    cos, s1, s2 = cos_ref[...], s1_ref[...], s2_ref[...]
    parts = []
    for c in range(qk_w // LANES):
        blk = qkn[:, c * LANES:(c + 1) * LANES]
        up = pltpu.roll(blk, LANES - ROPE_HALF, 1)
        dn = pltpu.roll(blk, ROPE_HALF, 1)
        parts.append(blk * cos + up * s1 + dn * s2)
    rot = jnp.concatenate(parts, axis=1)
    q_ref[0, 0] = rot[:, :q_w].T.astype(BF16)
    k_ref[...] = rot[:, q_w:qk_w].astype(BF16)
    v_ref[0, 0] = z[:, qk_w:qk_w + kv_w].T.astype(BF16)
    a = z[:, qk_w + kv_w:qk_w + kv_w + b_w]
    gate = z[:, qk_w + kv_w + b_w:qk_w + kv_w + 2 * b_w]
    glu_ref[...] = a * _sigmoid(gate)


def _even_in(x2d, g, w_in, q_norm, k_norm, seq):
    n, d = x2d.shape
    q_w, kv_w, b_w = A_HEADS * HEAD_DIM, A_KV_HEADS * HEAD_DIM, d // 2
    qk_w = q_w + kv_w
    tm = TM_PROJ
    gq = jnp.tile(q_norm.astype(F32), A_HEADS) * (HEAD_DIM ** -0.5 * LOG2E)
    gk = jnp.tile(k_norm.astype(F32), A_KV_HEADS)
    gqk = jnp.concatenate([gq, gk])[None, :]
    hid = jnp.arange(qk_w) // HEAD_DIM
    bd = (hid[:, None] == hid[None, :]).astype(BF16) * (1.0 / HEAD_DIM)
    pos = jnp.arange(seq, dtype=jnp.int32)
    row, col = pos // GRID_W, pos % GRID_W
    m = ROPE_HALF
    inv = ROPE_THETA ** (-jnp.arange(m, dtype=F32) / m)
    ang_r = row.astype(F32)[:, None] * inv[None, :]
    ang_c = col.astype(F32)[:, None] * inv[None, :]
    zero = jnp.zeros_like(ang_r)
    cos64 = jnp.concatenate([jnp.cos(ang_r), jnp.cos(ang_r), jnp.cos(ang_c), jnp.cos(ang_c)], axis=1)
    s1_64 = jnp.concatenate([-jnp.sin(ang_r), zero, -jnp.sin(ang_c), zero], axis=1)
    s2_64 = jnp.concatenate([zero, jnp.sin(ang_r), zero, jnp.sin(ang_c)], axis=1)
    rep = LANES // HEAD_DIM
    cos_t, s1_t, s2_t = (jnp.tile(t, (1, rep)) for t in (cos64, s1_64, s2_64))
    nseq = seq // tm
    tab_spec = pl.BlockSpec((tm, LANES), lambda i: (i % nseq, 0))
    full = lambda shape: pl.BlockSpec(shape, lambda i: (0,) * len(shape))
    kern = functools.partial(_even_in_kernel, qk_w=qk_w, q_w=q_w, kv_w=kv_w, b_w=b_w)
    return pl.pallas_call(
        kern,
        grid=(n // tm,),
        in_specs=[pl.BlockSpec((tm, d), lambda i: (i, 0)), full((1, d)), full(w_in.shape),
                  full((1, qk_w)), full((qk_w, qk_w)), tab_spec, tab_spec, tab_spec],
        out_specs=[_chunkT_spec(q_w, tm, nseq), pl.BlockSpec((tm, kv_w), lambda i: (i, 0)),
                   _chunkT_spec(kv_w, tm, nseq), pl.BlockSpec((tm, b_w), lambda i: (i, 0))],
        out_shape=[jax.ShapeDtypeStruct((n // seq, nseq, q_w, tm), BF16), jax.ShapeDtypeStruct((n, kv_w), BF16),
                   jax.ShapeDtypeStruct((n // seq, nseq, kv_w, tm), BF16), jax.ShapeDtypeStruct((n, b_w), F32)],
        compiler_params=_cparams("parallel"),
        name="even_in_proj",
    )(x2d, g[None, :].astype(F32), w_in.astype(BF16), gqk, bd, cos_t, s1_t, s2_t)


def _flash_pipelined(nk, n_maps, score_fn, pv_fn, s_a, s_b, tq, dv):
    assert nk % 2 == 0 and nk >= 2

    def scores(j, s_ref):
        cms = []
        for c in range(n_maps):
            s = score_fn(j, c)
            s_ref[c] = s
            cms.append(jnp.max(s, axis=0, keepdims=True))
        return tuple(cms)

    def softmax(j, s_ref, cms, state):
        new = []
        for c in range(n_maps):
            m, l, acc = state[c]
            m_new = jnp.maximum(m, cms[c])
            alpha = jnp.exp2(m - m_new)
            p = jnp.exp2(s_ref[c] - m_new)
            l = alpha * l + jnp.sum(p, axis=0, keepdims=True)
            acc = alpha * acc + pv_fn(j, c, p.astype(BF16))
            new.append((m_new, l, acc))
        return tuple(new)

    init = tuple((jnp.full((1, tq), NEG_BIG, F32), jnp.zeros((1, tq), F32), jnp.zeros((dv, tq), F32))
                 for _ in range(n_maps))
    cms0 = scores(0, s_a)

    def pair(t, carry):
        cms_a, state = carry
        j = 2 * t
        cms_b = scores(j + 1, s_b)
        state = softmax(j, s_a, cms_a, state)
        cms_a = scores(j + 2, s_a)
        state = softmax(j + 1, s_b, cms_b, state)
        return cms_a, state

    cms_a, state = lax.fori_loop(0, nk // 2 - 1, pair, (cms0, init))
    cms_b = scores(nk - 1, s_b)
    state = softmax(nk - 2, s_a, cms_a, state)
    state = softmax(nk - 1, s_b, cms_b, state)
    return tuple((l, acc) for _, l, acc in state)


def _gqa_kernel(qT_ref, k_ref, vT_ref, o_ref, s_a, s_b, *, nk, tk, nh):
    hd = HEAD_DIM
    tq = qT_ref.shape[3]
    kv = pl.program_id(1) // (A_GROUP // nh)
    row_head = lax.broadcasted_iota(jnp.int32, (A_KV_HEADS * hd, tq), 0) // hd
    qs = []
    for h in range(nh):
        qh = qT_ref[0, 0, h * hd:(h + 1) * hd, :]
        qs.append(jnp.where(row_head == kv, jnp.concatenate([qh] * A_KV_HEADS, axis=0), jnp.zeros((), BF16)))

    def score_fn(j, c):
        kc = k_ref[0, pl.ds(pl.multiple_of(j * tk, tk), tk), :]
        return jnp.dot(kc, qs[c], preferred_element_type=F32)

    pv_fn = lambda j, c, p: jnp.dot(vT_ref[0, j], p, preferred_element_type=F32)
    res = _flash_pipelined(nk, nh, score_fn, pv_fn, s_a, s_b, tq, hd)
    for h, (l, acc) in enumerate(res):
        o_ref[0, 0, h * hd:(h + 1) * hd, :] = (acc / l).astype(o_ref.dtype)


def _gqa_attention(qT, k, vT, batch, seq):
    hd, nh = HEAD_DIM, HEADS_PER_STEP_A
    tq = qT.shape[3]
    tk = vT.shape[3]
    nk = seq // tk
    hsteps = A_HEADS // nh
    per_kv = A_GROUP // nh
    k3 = k.reshape(batch, seq, A_KV_HEADS * hd)
    return pl.pallas_call(
        functools.partial(_gqa_kernel, nk=nk, tk=tk, nh=nh),
        grid=(batch, hsteps, seq // tq),
        in_specs=[pl.BlockSpec((1, 1, nh * hd, tq), lambda b, h, i: (b, i, h, 0)),
                  pl.BlockSpec((1, seq, A_KV_HEADS * hd), lambda b, h, i: (b, 0, 0)),
                  pl.BlockSpec((1, nk, hd, tk), lambda b, h, i: (b, 0, h // per_kv, 0))],
        out_specs=pl.BlockSpec((1, 1, nh * hd, tq), lambda b, h, i: (b, i, h, 0)),
        out_shape=jax.ShapeDtypeStruct(qT.shape, BF16),
        scratch_shapes=[pltpu.VMEM((nh, tk, tq), F32), pltpu.VMEM((nh, tk, tq), F32)],
        compiler_params=_cparams("parallel", "parallel", "parallel"),
        name="gqa_attention",
    )(qT, k3, vT)


def _conv_kernel(prev_ref, cur_ref, next_ref, w_ref, b_ref, g_ref, beta_ref, bd_ref, o_ref, xe_ref, *, ts):
    i = pl.program_id(1)
    last = pl.num_programs(1) - 1
    halo = CONV_HALO
    xe_ref[0:halo, :] = jnp.where(i > 0, prev_ref[0], 0.0)
    xe_ref[halo:halo + ts, :] = cur_ref[0]
    xe_ref[halo + ts:halo + ts + halo, :] = jnp.where(i < last, next_ref[0], 0.0)
    w = w_ref[...]
    first = halo - CONV_WIDTH // 2
    for r0 in range(0, ts, CONV_SUB):
        acc = jnp.zeros((CONV_SUB, w.shape[1]), F32) + b_ref[...]
        for j in range(CONV_WIDTH):
            acc = acc + w[j:j + 1, :] * xe_ref[r0 + first + j:r0 + first + j + CONV_SUB, :]
        mu = _split_dot(acc, bd_ref[...])
        cen = acc - mu
        var = _split_dot(cen * cen, bd_ref[...])
        hn = cen * lax.rsqrt(var + 1e-5) * g_ref[...] + beta_ref[...]
        o_ref[0, r0:r0 + CONV_SUB, :] = (hn * _sigmoid(hn)).astype(o_ref.dtype)


def _conformer_conv(glu, conv_w, conv_b, gn_g, gn_b, batch, seq):
    c = glu.shape[1]
    ts, halo = TS_CONV, CONV_HALO
    x3 = glu.reshape(batch, seq, c)
    hpb = ts // halo
    nh = seq // halo
    gid = jnp.arange(c) // B_GROUP_WIDTH
    bd = (gid[:, None] == gid[None, :]).astype(BF16) * (1.0 / B_GROUP_WIDTH)
    wpad = jnp.zeros((32, c), F32).at[:CONV_WIDTH].set(conv_w.astype(F32))
    vec = lambda a: a[None, :].astype(F32)
    full = lambda shape: pl.BlockSpec(shape, lambda b, i: (0,) * len(shape))
    out = pl.pallas_call(
        functools.partial(_conv_kernel, ts=ts),
        grid=(batch, seq // ts),
        in_specs=[pl.BlockSpec((1, halo, c), lambda b, i: (b, jnp.maximum(i * hpb - 1, 0), 0)),
                  pl.BlockSpec((1, ts, c), lambda b, i: (b, i, 0)),
                  pl.BlockSpec((1, halo, c), lambda b, i: (b, jnp.minimum((i + 1) * hpb, nh - 1), 0)),
                  full((32, c)), full((1, c)), full((1, c)), full((1, c)), full((c, c))],
        out_specs=pl.BlockSpec((1, ts, c), lambda b, i: (b, i, 0)),
        out_shape=jax.ShapeDtypeStruct((batch, seq, c), BF16),
        scratch_shapes=[pltpu.VMEM((ts + 2 * halo, c), F32)],
        compiler_params=_cparams("parallel", "parallel"),
        name="conformer_conv",
    )(x3, x3, x3, wpad, vec(conv_b), vec(gn_g), vec(gn_b), bd)
    return out.reshape(batch * seq, c)


def _route(logits):
    lane = lax.broadcasted_iota(jnp.int32, logits.shape, 1)
    big = jnp.int32(1 << 20)
    ninf = jnp.float32(-jnp.inf)
    gmask = (lane >= MOE_EXPERTS) & (lane < MOE_EXPERTS + MOE_GROUPS)
    lg = jnp.where(gmask, logits, ninf)
    mg = jnp.max(lg, axis=-1, keepdims=True)
    sg = jnp.sum(jnp.exp(lg - mg), axis=-1, keepdims=True)
    g_val = 1.0 / sg
    g_idx = jnp.min(jnp.where(lg == mg, lane - MOE_EXPERTS, big), axis=-1, keepdims=True)
    emask = (lane < MOE_EXPERTS) & ((lane // MOE_PER_GROUP) == g_idx)
    le = jnp.where(emask, logits, ninf)
    m1 = jnp.max(le, axis=-1, keepdims=True)
    se = jnp.sum(jnp.exp(le - m1), axis=-1, keepdims=True)
    i1 = jnp.min(jnp.where(le == m1, lane, big), axis=-1, keepdims=True)
    le2 = jnp.where(lane == i1, ninf, le)
    m2 = jnp.max(le2, axis=-1, keepdims=True)
    i2 = jnp.min(jnp.where(le2 == m2, lane, big), axis=-1, keepdims=True)
    e1 = 1.0 / se
    e2 = jnp.exp(m2 - m1) / se
    norm = g_val / (e1 + e2)
    ids = jnp.where(lane == 0, i1, jnp.where(lane == 1, i2, 0))
    gates = jnp.where(lane == 0, e1 * norm, jnp.where(lane == 1, e2 * norm, 0.0))
    return ids, gates


def _out_router_kernel(*refs, has_yb):
    if has_yb:
        yT_ref, yb_ref, x_ref, wa_ref, wb_ref = refs[:5]
        rest = refs[5:]
    else:
        yT_ref, x_ref, wa_ref = refs[:3]
        rest = refs[3:]
    g_ref, wrh_ref, wrl_ref, br_ref, x1_ref, h_ref, ids_ref, gates_ref = rest
    ya = yT_ref[0, 0].astype(F32).T.astype(BF16)
    x1 = x_ref[...] + jnp.dot(ya, wa_ref[...], preferred_element_type=F32)
    if has_yb:
        x1 = x1 + jnp.dot(yb_ref[...], wb_ref[...], preferred_element_type=F32)
    x1_ref[...] = x1
    h = _rms_rows(x1, g_ref[...], EPS)
    _rows_to_tiles(h_ref, h)
    hi = h.astype(BF16)
    lo = (h - hi.astype(F32)).astype(BF16)
    logits = (jnp.dot(hi, wrh_ref[...], preferred_element_type=F32)
              + jnp.dot(hi, wrl_ref[...], preferred_element_type=F32)
              + jnp.dot(lo, wrh_ref[...], preferred_element_type=F32) + br_ref[...])
    ids, gates = _route(logits)
    ids_ref[...] = ids
    gates_ref[...] = gates


def _out_router(yT, yb, x2d, w_out, g_ffn, w_group, b_group, w_expert, b_expert):
    n, d = x2d.shape
    tm = yT.shape[3]
    nseq = yT.shape[1]
    wa = yT.shape[2]
    wb16 = w_out.astype(BF16)
    wr = jnp.zeros((d, LANES), F32).at[:, :MOE_EXPERTS].set(w_expert.astype(F32))
    wr = wr.at[:, MOE_EXPERTS:MOE_EXPERTS + MOE_GROUPS].set(w_group.astype(F32))
    wrh = wr.astype(BF16)
    wrl = (wr - wrh.astype(F32)).astype(BF16)
    br = jnp.zeros((1, LANES), F32).at[0, :MOE_EXPERTS].set(b_expert.astype(F32))
    br = br.at[0, MOE_EXPERTS:MOE_EXPERTS + MOE_GROUPS].set(b_group.astype(F32))
    full = lambda shape: pl.BlockSpec(shape, lambda i: (0,) * len(shape))
    row = lambda w: pl.BlockSpec((tm, w), lambda i: (i, 0))
    has_yb = yb is not None
    if has_yb:
        args = [yT, yb, x2d, wb16[:wa], wb16[wa:]]
        specs = [_chunkT_spec(wa, tm, nseq), row(yb.shape[1]), row(d), full((wa, d)), full((w_out.shape[0] - wa, d))]
    else:
        assert wa == w_out.shape[0]
        args = [yT, x2d, wb16]
        specs = [_chunkT_spec(wa, tm, nseq), row(d), full((wa, d))]
    return pl.pallas_call(
        functools.partial(_out_router_kernel, has_yb=has_yb),
        grid=(n // tm,),
        in_specs=specs + [full((1, d)), full((d, LANES)), full((d, LANES)), full((1, LANES))],
        out_specs=[row(d), pl.BlockSpec((tm * SUBLANES, LANES), lambda i: (i, 0)), row(LANES), row(LANES)],
        out_shape=[jax.ShapeDtypeStruct((n, d), F32), jax.ShapeDtypeStruct((n * SUBLANES, LANES), F32),
                   jax.ShapeDtypeStruct((n, LANES), jnp.int32), jax.ShapeDtypeStruct((n, LANES), F32)],
        compiler_params=_cparams("parallel"),
        name="out_proj_router",
    )(*args, g_ffn[None, :].astype(F32), wrh, wrl, br)


def _moe_dispatch(ids, gates, n):
    bm = MOE_BM
    a = n * MOE_TOPK
    p = a + MOE_EXPERTS * bm
    flat_e = ids.reshape(-1)
    flat_w = gates.reshape(-1)
    order = jnp.argsort(flat_e).astype(jnp.int32)
    experts = jnp.arange(MOE_EXPERTS, dtype=jnp.int32)
    counts = jnp.sum((flat_e[:, None] == experts[None, :]).astype(jnp.int32), axis=0)
    pcounts = (counts + bm - 1) // bm * bm
    starts = jnp.cumsum(counts) - counts
    pends = jnp.cumsum(pcounts)
    pstarts = pends - pcounts
    n_blk = p // bm
    blk_start = jnp.arange(n_blk, dtype=jnp.int32) * bm
    blk_e = jnp.minimum(jnp.sum((blk_start[:, None] >= pends[None, :]).astype(jnp.int32), axis=1),
                        MOE_EXPERTS - 1)
    off = (blk_start - pstarts[blk_e])[:, None] + jnp.arange(bm, dtype=jnp.int32)[None, :]
    valid = (off < counts[blk_e][:, None]).reshape(p)
    sidx = jnp.clip(starts[blk_e][:, None] + off, 0, a - 1).reshape(p)
    assign = order[sidx]
    tok = assign // MOE_TOPK
    slot = assign % MOE_TOPK
    src_tok = jnp.where(valid, tok, 0)
    dump = a + jnp.cumsum(1 - valid.astype(jnp.int32)) - 1
    dst_row = jnp.where(valid, slot * n + tok, dump).astype(jnp.int32)
    row_w = jnp.where(valid, flat_w[assign], 0.0)
    tok_bits = (n - 1).bit_length()
    assert tok_bits + (p - 1).bit_length() <= 32
    packed = lax.bitcast_convert_type(
        src_tok.astype(jnp.uint32) | (dst_row.astype(jnp.uint32) << tok_bits), jnp.int32)
    return blk_e, packed, row_w[:, None], tok_bits


def _moe_kernel(be_ref, idx_ref, h_hbm, w1_ref, w3_ref, w2_ref, rw_ref, y_hbm,
                xbuf, obuf, gsem, ssem, *, tok_bits):
    del be_ref
    bm = MOE_BM
    i = pl.program_id(0)
    nb = pl.num_programs(0)
    slot = i % 2
    tok_mask = (1 << tok_bits) - 1

    tile = lambda r: pl.ds(pl.multiple_of(r * SUBLANES, SUBLANES), SUBLANES)

    def row_loop(body):
        lax.fori_loop(0, bm, lambda r, c: (body(r), c)[1], 0, unroll=MOE_DMA_UNROLL)

    def gather_start(blk, sl):
        def body(r):
            t = idx_ref[blk * bm + r] & tok_mask
            pltpu.make_async_copy(h_hbm.at[tile(t)], xbuf.at[sl, tile(r)], gsem.at[sl]).start()
        row_loop(body)

    def gather_wait(sl):
        pltpu.make_async_copy(h_hbm.at[pl.ds(0, bm * SUBLANES)], xbuf.at[sl], gsem.at[sl]).wait()

    def scatter_start(blk, sl):
        def body(r):
            t = lax.shift_right_logical(idx_ref[blk * bm + r], tok_bits)
            pltpu.make_async_copy(obuf.at[sl, tile(r)], y_hbm.at[tile(t)], ssem.at[sl]).start()
        row_loop(body)

    def scatter_wait(sl):
        pltpu.make_async_copy(obuf.at[sl], y_hbm.at[pl.ds(0, bm * SUBLANES)], ssem.at[sl]).wait()

    @pl.when(i == 0)
    def _():
        gather_start(0, 0)

    @pl.when(i + 1 < nb)
    def _():
        gather_start(i + 1, 1 - slot)

    gather_wait(slot)
    x = _rows_from_tiles(xbuf.at[slot]).astype(BF16)
    h1 = jnp.dot(x, w1_ref[0].astype(BF16), preferred_element_type=F32)
    h3 = jnp.dot(x, w3_ref[0].astype(BF16), preferred_element_type=F32)
    hm = (h1 * _sigmoid(h1) * h3).astype(BF16)
    y = jnp.dot(hm, w2_ref[0].astype(BF16), preferred_element_type=F32) * rw_ref[...]

    @pl.when(i >= 2)
    def _():
        scatter_wait(slot)

    _rows_to_tiles(obuf.at[slot], y)
    scatter_start(i, slot)

    @pl.when(i == nb - 1)
    def _():
        scatter_wait(slot)
        scatter_wait(1 - slot)


def _moe_ffn(h_tiles, ids, gates, w1, w3, w2):
    d = SUBLANES * LANES
    n = h_tiles.shape[0] // SUBLANES
    assert w1.shape[1] == d and h_tiles.shape[1] == LANES
    bm = MOE_BM
    hid = w1.shape[-1]
    blk_e, packed, row_w, tok_bits = _moe_dispatch(ids, gates, n)
    n_blk = blk_e.shape[0]
    rows_out = n * MOE_TOPK + MOE_EXPERTS * bm
    grid_spec = pltpu.PrefetchScalarGridSpec(
        num_scalar_prefetch=2,
        grid=(n_blk,),
        in_specs=[pl.BlockSpec(memory_space=pl.ANY),
                  pl.BlockSpec((1, d, hid), lambda i, be, ix: (be[i], 0, 0)),
                  pl.BlockSpec((1, d, hid), lambda i, be, ix: (be[i], 0, 0)),
                  pl.BlockSpec((1, hid, d), lambda i, be, ix: (be[i], 0, 0)),
                  pl.BlockSpec((bm, 1), lambda i, be, ix: (i, 0))],
        out_specs=pl.BlockSpec(memory_space=pl.ANY),
        scratch_shapes=[pltpu.VMEM((2, bm * SUBLANES, LANES), F32), pltpu.VMEM((2, bm * SUBLANES, LANES), F32),
                        pltpu.SemaphoreType.DMA((2,)), pltpu.SemaphoreType.DMA((2,))],
    )
    return pl.pallas_call(
        functools.partial(_moe_kernel, tok_bits=tok_bits),
        grid_spec=grid_spec,
        out_shape=jax.ShapeDtypeStruct((rows_out * SUBLANES, LANES), F32),
        compiler_params=_cparams("arbitrary"),
        name="moe_ffn",
    )(blk_e, packed, h_tiles, w1, w3, w2, row_w)


def _odd_in_kernel(x_ref, y0_ref, y1_ref, g_ref, w_ref, xo_ref, q_ref, k_ref, v_ref, *, cw):
    x = x_ref[...] + _rows_from_tiles(y0_ref) + _rows_from_tiles(y1_ref)
    xo_ref[...] = x
    h = _rms_rows(x, g_ref[...], EPS)
    z = jnp.dot(h.astype(BF16), w_ref[...], preferred_element_type=F32)
    q_ref[0, 0] = (z[:, :cw] * (C_HEAD_DIM ** -0.5 * LOG2E)).T.astype(BF16)
    k_ref[...] = z[:, cw:2 * cw].astype(BF16)
    v_ref[0, 0] = z[:, 2 * cw:3 * cw].T.astype(BF16)


def _odd_in(x2d, ybuf, g, w_in, seq):
    n, d = x2d.shape
    tm = TM_PROJ
    cw = w_in.shape[1] // 3
    nt = n // tm
    row = lambda w: pl.BlockSpec((tm, w), lambda i: (i, 0))
    full = lambda shape: pl.BlockSpec(shape, lambda i: (0,) * len(shape))
    return pl.pallas_call(
        functools.partial(_odd_in_kernel, cw=cw),
        grid=(nt,),
        in_specs=[row(d), pl.BlockSpec((tm * SUBLANES, LANES), lambda i: (i, 0)),
                  pl.BlockSpec((tm * SUBLANES, LANES), lambda i: (i + nt, 0)), full((1, d)), full(w_in.shape)],
        out_specs=[row(d), _chunkT_spec(cw, tm, seq // tm), row(cw), _chunkT_spec(cw, tm, seq // tm)],
        out_shape=[jax.ShapeDtypeStruct((n, d), F32), jax.ShapeDtypeStruct((n // seq, seq // tm, cw, tm), BF16),
                   jax.ShapeDtypeStruct((n, cw), BF16), jax.ShapeDtypeStruct((n // seq, seq // tm, cw, tm), BF16)],
        compiler_params=_cparams("parallel"),
        name="odd_in_proj",
    )(x2d, ybuf, ybuf, g[None, :].astype(F32), w_in.astype(BF16))


def _t5_bucket(rel):
    nb = REL_BUCKETS // 2
    max_exact = nb // 2
    ret = jnp.where(rel > 0, nb, 0).astype(jnp.int32)
    n = jnp.abs(rel)
    nf = jnp.maximum(n, 1).astype(F32)
    large = max_exact + (jnp.log(nf / max_exact) / math.log(REL_MAX_DIST / max_exact) * (nb - max_exact)).astype(jnp.int32)
    large = jnp.minimum(large, nb - 1)
    return ret + jnp.where(n < max_exact, n, large)


def _diff_kernel(lam_ref, qT_ref, k_ref, vT_ref, band_ref, gsub_ref, o_ref, s_a, s_b, *, nk, tk):
    i = pl.program_id(2)
    hd = C_HEAD_DIM
    tq = qT_ref.shape[3]
    dv = vT_ref.shape[2]
    qT = qT_ref[0, 0]
    row_map = lax.broadcasted_iota(jnp.int32, qT.shape, 0) // hd
    qs = [jnp.where(row_map == c, qT, jnp.zeros((), BF16)) for c in range(2)]

    def score_fn(j, c):
        bias = band_ref[0, jnp.clip(j - i, -2, 2) + 2]
        kc = k_ref[0, pl.ds(pl.multiple_of(j * tk, tk), tk), :]
        return jnp.dot(kc, qs[c], preferred_element_type=F32) + bias

    pv_fn = lambda j, c, p: jnp.dot(vT_ref[0, j], p, preferred_element_type=F32)
    (l0, a0), (l1, a1) = _flash_pipelined(nk, 2, score_fn, pv_fn, s_a, s_b, tq, dv)
    o = a0 / l0 - lam_ref[...] * (a1 / l1)
    ms = jnp.mean(o * o, axis=0, keepdims=True)
    o_ref[0, 0] = (o * lax.rsqrt(ms + 1e-5) * gsub_ref[...]).astype(o_ref.dtype)


def _bias_band(rel_bias, t):
    assert t >= REL_MAX_DIST
    rb = rel_bias.astype(F32) * LOG2E
    period = 2 * t
    m = jnp.arange(period, dtype=jnp.int32)
    d3 = jnp.arange(-1, 2, dtype=jnp.int32)[:, None] * t
    rel = jnp.where(m[None, :] < t, d3 - m[None, :], d3 + (period - m[None, :]))
    vec = rb[_t5_bucket(rel)].transpose(2, 0, 1)
    nh = vec.shape[0]
    skew = jnp.tile(vec, (1, 1, t))[:, :, :t * (period - 1)].reshape(nh, 3, t, period - 1)[..., :t]
    far = lambda r: jnp.broadcast_to(rb[_t5_bucket(jnp.int32(r))][:, None, None, None], (nh, 1, t, t))
    return jnp.concatenate([far(-2 * t), skew, far(2 * t)], axis=1)


def _diff_attention(qT, k, vT, lam, lam_init, subln, rel_bias, batch, seq):
    hd = C_HEAD_DIM
    dv = 2 * hd
    tq = qT.shape[3]
    tk = vT.shape[3]
    nk = seq // tk
    assert tk == tq
    k3 = k.reshape(batch, seq, C_HEADS * dv)
    band = _bias_band(rel_bias, tk)
    lam_vec = jnp.full((1, tq), lam, F32)
    gsub = (subln.astype(F32) * (1.0 - lam_init))[:, None]
    return pl.pallas_call(
        functools.partial(_diff_kernel, nk=nk, tk=tk),
        grid=(batch, C_HEADS, seq // tq),
        in_specs=[pl.BlockSpec((1, tq), lambda b, h, i: (0, 0)),
                  pl.BlockSpec((1, 1, dv, tq), lambda b, h, i: (b, i, h, 0)),
                  pl.BlockSpec((1, seq, dv), lambda b, h, i: (b, 0, h)),
                  pl.BlockSpec((1, nk, dv, tk), lambda b, h, i: (b, 0, h, 0)),
                  pl.BlockSpec((1, 5, tk, tq), lambda b, h, i: (h, 0, 0, 0)),
                  pl.BlockSpec((dv, 1), lambda b, h, i: (0, 0))],
        out_specs=pl.BlockSpec((1, 1, dv, tq), lambda b, h, i: (b, i, h, 0)),
        out_shape=jax.ShapeDtypeStruct(qT.shape, BF16),
        scratch_shapes=[pltpu.VMEM((2, tk, tq), F32), pltpu.VMEM((2, tk, tq), F32)],
        compiler_params=_cparams("parallel", "parallel", "parallel"),
        name="diff_attention",
    )(lam_vec, qT, k3, vT, band, gsub)


def _final_kernel(x_ref, y0_ref, y1_ref, g_ref, o_ref):
    x = x_ref[...] + _rows_from_tiles(y0_ref) + _rows_from_tiles(y1_ref)
    o_ref[...] = _rms_rows(x, g_ref[...], EPS)


def _final(x2d, ybuf, g):
    n, d = x2d.shape
    tm = TM_PROJ
    nt = n // tm
    row = pl.BlockSpec((tm, d), lambda i: (i, 0))
    return pl.pallas_call(
        _final_kernel,
        grid=(nt,),
        in_specs=[row, pl.BlockSpec((tm * SUBLANES, LANES), lambda i: (i, 0)),
                  pl.BlockSpec((tm * SUBLANES, LANES), lambda i: (i + nt, 0)), pl.BlockSpec((1, d), lambda i: (0, 0))],
        out_specs=row,
        out_shape=jax.ShapeDtypeStruct((n, d), F32),
        compiler_params=_cparams("parallel"),
        name="final_norm",
    )(x2d, ybuf, ybuf, g[None, :].astype(F32))


def kernel(x, norm_mix, norm_ffn, norm_final, ev_w_in, ev_q_norm, ev_k_norm, ev_conv_w, ev_conv_b, ev_gn_g, ev_gn_b, ev_w_out, od_w_in, od_lam_q1, od_lam_k1, od_lam_q2, od_lam_k2, od_subln, od_w_out, rel_bias, moe_w_group, moe_b_group, moe_w_expert, moe_b_expert, moe_w1, moe_w3, moe_w2):
    batch, seq, d = x.shape
    n = batch * seq
    x2d = x.reshape(n, d).astype(F32)

    qT, k, vT, glu = _even_in(x2d, norm_mix[0], ev_w_in[0], ev_q_norm[0], ev_k_norm[0], seq)
    yaT = _gqa_attention(qT, k, vT, batch, seq)
    yb = _conformer_conv(glu, ev_conv_w[0], ev_conv_b[0], ev_gn_g[0], ev_gn_b[0], batch, seq)
    x1, h, ids, gates = _out_router(yaT, yb, x2d, ev_w_out[0], norm_ffn[0],
                                    moe_w_group[0], moe_b_group[0], moe_w_expert[0], moe_b_expert[0])
    ybuf = _moe_ffn(h, ids[:, :MOE_TOPK], gates[:, :MOE_TOPK], moe_w1[0], moe_w3[0], moe_w2[0])

    x2, qT, k, vT = _odd_in(x1, ybuf, norm_mix[1], od_w_in[0], seq)
    layer_idx = 1
    lam_init = 0.8 - 0.6 * math.exp(-0.3 * layer_idx)
    lam = (jnp.exp(jnp.sum(od_lam_q1[0].astype(F32) * od_lam_k1[0].astype(F32)))
           - jnp.exp(jnp.sum(od_lam_q2[0].astype(F32) * od_lam_k2[0].astype(F32))) + lam_init)
    oT = _diff_attention(qT, k, vT, lam, lam_init, od_subln[0], rel_bias, batch, seq)
    x3, h, ids, gates = _out_router(oT, None, x2, od_w_out[0], norm_ffn[1],
                                    moe_w_group[1], moe_b_group[1], moe_w_expert[1], moe_b_expert[1])
    ybuf = _moe_ffn(h, ids[:, :MOE_TOPK], gates[:, :MOE_TOPK], moe_w1[1], moe_w3[1], moe_w2[1])
    out = _final(x3, ybuf, norm_final)
    return out.reshape(batch, seq, d)
```

```python
import functools
import math

import jax
import jax.numpy as jnp
from jax import lax
from jax.experimental import pallas as pl
from jax.experimental.pallas import tpu as pltpu

F32 = jnp.float32
BF16 = jnp.bfloat16

GRID_W = 64
HEAD_DIM = 64
EPS = 1e-6
A_HEADS = 8
A_KV_HEADS = 2
A_GROUP = A_HEADS // A_KV_HEADS
ROPE_THETA = 10000.0
ROPE_HALF = HEAD_DIM // 4
B_GROUP_WIDTH = 64
CONV_WIDTH = 31
CONV_HALO = 16
C_HEADS = 8
C_HEAD_DIM = 64
REL_BUCKETS = 32
REL_MAX_DIST = 128
MOE_GROUPS = 4
MOE_PER_GROUP = 8
MOE_EXPERTS = MOE_GROUPS * MOE_PER_GROUP
MOE_TOPK = 2
LOG2E = 1.4426950408889634
NEG_BIG = -1e30
LANES = 128
SUBLANES = 8
MOE_DMA_UNROLL = 8

TM_PROJ = 512
HEADS_PER_STEP_A = 2
DIFF_LEAD = 4
BF16_ROWS = 16
TS_CONV = 256
CONV_SUB = 64
MOE_BM = 128
VMEM_LIMIT = 56 * 1024 * 1024


def _cparams(*sem):
    return pltpu.CompilerParams(dimension_semantics=sem, vmem_limit_bytes=VMEM_LIMIT)


def _split_dot(a, b_bf16):
    hi = a.astype(BF16)
    lo = (a - hi.astype(F32)).astype(BF16)
    return (jnp.dot(hi, b_bf16, preferred_element_type=F32)
            + jnp.dot(lo, b_bf16, preferred_element_type=F32))


def _rms_rows(x, g, eps):
    ms = jnp.mean(x * x, axis=-1, keepdims=True)
    return x * lax.rsqrt(ms + eps) * g


def _sigmoid(x):
    return 1.0 / (1.0 + jnp.exp(-x))


def _chunkT_spec(width, tm, nseq):
    return pl.BlockSpec((1, 1, width, tm), lambda i: (i // nseq, i % nseq, 0, 0))


def _rows_from_tiles(ref):
    rows = ref.shape[0] // SUBLANES
    return jnp.concatenate([ref[pl.ds(s, rows, stride=SUBLANES), :] for s in range(SUBLANES)], axis=1)


def _rows_to_tiles(ref, val):
    rows = val.shape[0]
    for s in range(SUBLANES):
        ref[pl.ds(s, rows, stride=SUBLANES), :] = val[:, s * LANES:(s + 1) * LANES]


def _even_in_kernel(x_ref, g_ref, w_ref, gqk_ref, bd_ref, cos_ref, s1_ref, s2_ref,
                    q_ref, k_ref, v_ref, glu_ref, *, qk_w, q_w, kv_w, b_w):
    h = _rms_rows(x_ref[...], g_ref[...], EPS)
    z = jnp.dot(h.astype(BF16), w_ref[...], preferred_element_type=F32)
    qk = z[:, :qk_w]
    ms = _split_dot(qk * qk, bd_ref[...])
    qkn = qk * lax.rsqrt(ms + EPS) * gqk_ref[...]
    cos, s1, s2 = cos_ref[...], s1_ref[...], s2_ref[...]
    parts = []
    for c in range(qk_w // LANES):
        blk = qkn[:, c * LANES:(c + 1) * LANES]
        up = pltpu.roll(blk, LANES - ROPE_HALF, 1)
        dn = pltpu.roll(blk, ROPE_HALF, 1)
        parts.append(blk * cos + up * s1 + dn * s2)
    rot = jnp.concatenate(parts, axis=1)
    q_ref[0, 0] = rot[:, :q_w].T.astype(BF16)
    k_ref[...] = rot[:, q_w:qk_w].astype(BF16)
    v_ref[0, 0] = z[:, qk_w:qk_w + kv_w].T.astype(BF16)
    a = z[:, qk_w + kv_w:qk_w + kv_w + b_w]
    gate = z[:, qk_w + kv_w + b_w:qk_w + kv_w + 2 * b_w]
    glu_ref[...] = a * _sigmoid(gate)


def _even_in(x2d, g, w_in, q_norm, k_norm, seq):
    n, d = x2d.shape
    q_w, kv_w, b_w = A_HEADS * HEAD_DIM, A_KV_HEADS * HEAD_DIM, d // 2
    qk_w = q_w + kv_w
    tm = TM_PROJ
    gq = jnp.tile(q_norm.astype(F32), A_HEADS) * (HEAD_DIM ** -0.5 * LOG2E)
    gk = jnp.tile(k_norm.astype(F32), A_KV_HEADS)
    gqk = jnp.concatenate([gq, gk])[None, :]
    hid = jnp.arange(qk_w) // HEAD_DIM
    bd = (hid[:, None] == hid[None, :]).astype(BF16) * (1.0 / HEAD_DIM)
    pos = jnp.arange(seq, dtype=jnp.int32)
    row, col = pos // GRID_W, pos % GRID_W
    m = ROPE_HALF
    inv = ROPE_THETA ** (-jnp.arange(m, dtype=F32) / m)
    ang_r = row.astype(F32)[:, None] * inv[None, :]
    ang_c = col.astype(F32)[:, None] * inv[None, :]
    zero = jnp.zeros_like(ang_r)
    cos64 = jnp.concatenate([jnp.cos(ang_r), jnp.cos(ang_r), jnp.cos(ang_c), jnp.cos(ang_c)], axis=1)
    s1_64 = jnp.concatenate([-jnp.sin(ang_r), zero, -jnp.sin(ang_c), zero], axis=1)
    s2_64 = jnp.concatenate([zero, jnp.sin(ang_r), zero, jnp.sin(ang_c)], axis=1)
    rep = LANES // HEAD_DIM
    cos_t, s1_t, s2_t = (jnp.tile(t, (1, rep)) for t in (cos64, s1_64, s2_64))
    nseq = seq // tm
    tab_spec = pl.BlockSpec((tm, LANES), lambda i: (i % nseq, 0))
    full = lambda shape: pl.BlockSpec(shape, lambda i: (0,) * len(shape))
    kern = functools.partial(_even_in_kernel, qk_w=qk_w, q_w=q_w, kv_w=kv_w, b_w=b_w)
    return pl.pallas_call(
        kern,
        grid=(n // tm,),
        in_specs=[pl.BlockSpec((tm, d), lambda i: (i, 0)), full((1, d)), full(w_in.shape),
                  full((1, qk_w)), full((qk_w, qk_w)), tab_spec, tab_spec, tab_spec],
        out_specs=[_chunkT_spec(q_w, tm, nseq), pl.BlockSpec((tm, kv_w), lambda i: (i, 0)),
                   _chunkT_spec(kv_w, tm, nseq), pl.BlockSpec((tm, b_w), lambda i: (i, 0))],
        out_shape=[jax.ShapeDtypeStruct((n // seq, nseq, q_w, tm), BF16), jax.ShapeDtypeStruct((n, kv_w), BF16),
                   jax.ShapeDtypeStruct((n // seq, nseq, kv_w, tm), BF16), jax.ShapeDtypeStruct((n, b_w), F32)],
        compiler_params=_cparams("parallel"),
        name="even_in_proj",
    )(x2d, g[None, :].astype(F32), w_in.astype(BF16), gqk, bd, cos_t, s1_t, s2_t)


def _flash_pipelined(nk, n_lead, n_maps, score_fn, shift_fn, value_fn, s_a, s_b, tq, dv):
    assert nk % 2 == 0 and n_lead % 2 == 0 and nk - n_lead >= 2
    ones = jnp.ones((BF16_ROWS, s_a.shape[1]), BF16)

    def scores(t, s_ref, lead):
        shift = shift_fn(t, lead)
        cms = []
        for c in range(n_maps):
            s = score_fn(t, c, lead)
            s_ref[c] = s
            cm = jnp.max(s, axis=0, keepdims=True)
            cms.append(cm if shift is None else cm + shift)
        return tuple(cms)

    def softmax(t, s_ref, cms, state, lead):
        shift = shift_fn(t, lead)
        vals = jnp.concatenate([value_fn(t), ones], axis=0)
        new = []
        for c in range(n_maps):
            m, acc = state[c]
            m_new = jnp.maximum(m, cms[c])
            alpha = jnp.exp2(m - m_new)
            p = jnp.exp2(s_ref[c] - (m_new if shift is None else m_new - shift))
            acc = alpha * acc + jnp.dot(vals, p.astype(BF16), preferred_element_type=F32)
            new.append((m_new, acc))
        return tuple(new)

    state = tuple((jnp.full((1, tq), NEG_BIG, F32), jnp.zeros((dv + BF16_ROWS, tq), F32)) for _ in range(n_maps))
    bufs = (s_a, s_b)
    cms = scores(0, s_a, n_lead > 0)
    for t in range(n_lead):
        nxt = scores(t + 1, bufs[(t + 1) % 2], t + 1 < n_lead)
        state = softmax(t, bufs[t % 2], cms, state, True)
        cms = nxt

    def pair(u, carry):
        cms_a, state = carry
        t = n_lead + 2 * u
        cms_b = scores(t + 1, s_b, False)
        state = softmax(t, s_a, cms_a, state, False)
        cms_a = scores(t + 2, s_a, False)
        state = softmax(t + 1, s_b, cms_b, state, False)
        return cms_a, state

    cms_a, state = lax.fori_loop(0, (nk - n_lead) // 2 - 1, pair, (cms, state))
    cms_b = scores(nk - 1, s_b, False)
    state = softmax(nk - 2, s_a, cms_a, state, False)
    state = softmax(nk - 1, s_b, cms_b, state, False)
    return tuple(acc[:dv] / acc[dv:dv + 1] for _, acc in state)


def _gqa_kernel(qT_ref, k_ref, vT_ref, o_ref, s_a, s_b, *, nk, tk, nh):
    hd = HEAD_DIM
    tq = qT_ref.shape[3]
    kv = pl.program_id(1) // (A_GROUP // nh)
    row_head = lax.broadcasted_iota(jnp.int32, (A_KV_HEADS * hd, tq), 0) // hd
    qs = []
    for h in range(nh):
        qh = qT_ref[0, 0, h * hd:(h + 1) * hd, :]
        qs.append(jnp.where(row_head == kv, jnp.concatenate([qh] * A_KV_HEADS, axis=0), jnp.zeros((), BF16)))

    def score_fn(t, c, lead):
        kc = k_ref[0, pl.ds(pl.multiple_of(t * tk, tk), tk), :]
        return jnp.dot(kc, qs[c], preferred_element_type=F32)

    res = _flash_pipelined(nk, 0, nh, score_fn, lambda t, lead: None, lambda t: vT_ref[0, t], s_a, s_b, tq, hd)
    for h, o in enumerate(res):
        o_ref[0, 0, h * hd:(h + 1) * hd, :] = o.astype(o_ref.dtype)


def _gqa_attention(qT, k, vT, batch, seq):
    hd, nh = HEAD_DIM, HEADS_PER_STEP_A
    tq = qT.shape[3]
    tk = vT.shape[3]
    nk = seq // tk
    hsteps = A_HEADS // nh
    per_kv = A_GROUP // nh
    k3 = k.reshape(batch, seq, A_KV_HEADS * hd)
    return pl.pallas_call(
        functools.partial(_gqa_kernel, nk=nk, tk=tk, nh=nh),
        grid=(batch, hsteps, seq // tq),
        in_specs=[pl.BlockSpec((1, 1, nh * hd, tq), lambda b, h, i: (b, i, h, 0)),
                  pl.BlockSpec((1, seq, A_KV_HEADS * hd), lambda b, h, i: (b, 0, 0)),
                  pl.BlockSpec((1, nk, hd, tk), lambda b, h, i: (b, 0, h // per_kv, 0))],
        out_specs=pl.BlockSpec((1, 1, nh * hd, tq), lambda b, h, i: (b, i, h, 0)),
        out_shape=jax.ShapeDtypeStruct(qT.shape, BF16),
        scratch_shapes=[pltpu.VMEM((nh, tk, tq), F32), pltpu.VMEM((nh, tk, tq), F32)],
        compiler_params=_cparams("parallel", "parallel", "parallel"),
        name="gqa_attention",
    )(qT, k3, vT)


def _conv_kernel(prev_ref, cur_ref, next_ref, w_ref, b_ref, g_ref, beta_ref, bd_ref, o_ref, xe_ref, *, ts):
    i = pl.program_id(1)
    last = pl.num_programs(1) - 1
    halo = CONV_HALO
    xe_ref[0:halo, :] = jnp.where(i > 0, prev_ref[0], 0.0)
    xe_ref[halo:halo + ts, :] = cur_ref[0]
    xe_ref[halo + ts:halo + ts + halo, :] = jnp.where(i < last, next_ref[0], 0.0)
    w = w_ref[...]
    first = halo - CONV_WIDTH // 2
    for r0 in range(0, ts, CONV_SUB):
        acc = jnp.zeros((CONV_SUB, w.shape[1]), F32) + b_ref[...]
        for j in range(CONV_WIDTH):
            acc = acc + w[j:j + 1, :] * xe_ref[r0 + first + j:r0 + first + j + CONV_SUB, :]
        mu = _split_dot(acc, bd_ref[...])
        cen = acc - mu
        var = _split_dot(cen * cen, bd_ref[...])
        hn = cen * lax.rsqrt(var + 1e-5) * g_ref[...] + beta_ref[...]
        o_ref[0, r0:r0 + CONV_SUB, :] = (hn * _sigmoid(hn)).astype(o_ref.dtype)


def _conformer_conv(glu, conv_w, conv_b, gn_g, gn_b, batch, seq):
    c = glu.shape[1]
    ts, halo = TS_CONV, CONV_HALO
    x3 = glu.reshape(batch, seq, c)
    hpb = ts // halo
    nh = seq // halo
    gid = jnp.arange(c) // B_GROUP_WIDTH
    bd = (gid[:, None] == gid[None, :]).astype(BF16) * (1.0 / B_GROUP_WIDTH)
    wpad = jnp.zeros((32, c), F32).at[:CONV_WIDTH].set(conv_w.astype(F32))
    vec = lambda a: a[None, :].astype(F32)
    full = lambda shape: pl.BlockSpec(shape, lambda b, i: (0,) * len(shape))
    out = pl.pallas_call(
        functools.partial(_conv_kernel, ts=ts),
        grid=(batch, seq // ts),
        in_specs=[pl.BlockSpec((1, halo, c), lambda b, i: (b, jnp.maximum(i * hpb - 1, 0), 0)),
                  pl.BlockSpec((1, ts, c), lambda b, i: (b, i, 0)),
                  pl.BlockSpec((1, halo, c), lambda b, i: (b, jnp.minimum((i + 1) * hpb, nh - 1), 0)),
                  full((32, c)), full((1, c)), full((1, c)), full((1, c)), full((c, c))],
        out_specs=pl.BlockSpec((1, ts, c), lambda b, i: (b, i, 0)),
        out_shape=jax.ShapeDtypeStruct((batch, seq, c), BF16),
        scratch_shapes=[pltpu.VMEM((ts + 2 * halo, c), F32)],
        compiler_params=_cparams("parallel", "parallel"),
        name="conformer_conv",
    )(x3, x3, x3, wpad, vec(conv_b), vec(gn_g), vec(gn_b), bd)
    return out.reshape(batch * seq, c)


def _route(logits):
    lane = lax.broadcasted_iota(jnp.int32, logits.shape, 1)
    big = jnp.int32(1 << 20)
    ninf = jnp.float32(-jnp.inf)
    gmask = (lane >= MOE_EXPERTS) & (lane < MOE_EXPERTS + MOE_GROUPS)
    lg = jnp.where(gmask, logits, ninf)
    mg = jnp.max(lg, axis=-1, keepdims=True)
    sg = jnp.sum(jnp.exp(lg - mg), axis=-1, keepdims=True)
    g_val = 1.0 / sg
    g_idx = jnp.min(jnp.where(lg == mg, lane - MOE_EXPERTS, big), axis=-1, keepdims=True)
    emask = (lane < MOE_EXPERTS) & ((lane // MOE_PER_GROUP) == g_idx)
    le = jnp.where(emask, logits, ninf)
    m1 = jnp.max(le, axis=-1, keepdims=True)
    se = jnp.sum(jnp.exp(le - m1), axis=-1, keepdims=True)
    i1 = jnp.min(jnp.where(le == m1, lane, big), axis=-1, keepdims=True)
    le2 = jnp.where(lane == i1, ninf, le)
    m2 = jnp.max(le2, axis=-1, keepdims=True)
    i2 = jnp.min(jnp.where(le2 == m2, lane, big), axis=-1, keepdims=True)
    e1 = 1.0 / se
    e2 = jnp.exp(m2 - m1) / se
    norm = g_val / (e1 + e2)
    ids = jnp.where(lane == 0, i1, jnp.where(lane == 1, i2, 0))
    gates = jnp.where(lane == 0, e1 * norm, jnp.where(lane == 1, e2 * norm, 0.0))
    return ids, gates


def _out_router_kernel(*refs, has_yb):
    if has_yb:
        yT_ref, yb_ref, x_ref, wa_ref, wb_ref = refs[:5]
        rest = refs[5:]
    else:
        yT_ref, x_ref, wa_ref = refs[:3]
        rest = refs[3:]
    g_ref, wrh_ref, wrl_ref, br_ref, x1_ref, h_ref, ids_ref, gates_ref = rest
    ya = yT_ref[0, 0].astype(F32).T.astype(BF16)
    x1 = x_ref[...] + jnp.dot(ya, wa_ref[...], preferred_element_type=F32)
    if has_yb:
        x1 = x1 + jnp.dot(yb_ref[...], wb_ref[...], preferred_element_type=F32)
    x1_ref[...] = x1
    h = _rms_rows(x1, g_ref[...], EPS)
    _rows_to_tiles(h_ref, h)
    hi = h.astype(BF16)
    lo = (h - hi.astype(F32)).astype(BF16)
    logits = (jnp.dot(hi, wrh_ref[...], preferred_element_type=F32)
              + jnp.dot(hi, wrl_ref[...], preferred_element_type=F32)
              + jnp.dot(lo, wrh_ref[...], preferred_element_type=F32) + br_ref[...])
    ids, gates = _route(logits)
    ids_ref[...] = ids
    gates_ref[...] = gates


def _out_router(yT, yb, x2d, w_out, g_ffn, w_group, b_group, w_expert, b_expert):
    n, d = x2d.shape
    tm = yT.shape[3]
    nseq = yT.shape[1]
    wa = yT.shape[2]
    wb16 = w_out.astype(BF16)
    wr = jnp.zeros((d, LANES), F32).at[:, :MOE_EXPERTS].set(w_expert.astype(F32))
    wr = wr.at[:, MOE_EXPERTS:MOE_EXPERTS + MOE_GROUPS].set(w_group.astype(F32))
    wrh = wr.astype(BF16)
    wrl = (wr - wrh.astype(F32)).astype(BF16)
    br = jnp.zeros((1, LANES), F32).at[0, :MOE_EXPERTS].set(b_expert.astype(F32))
    br = br.at[0, MOE_EXPERTS:MOE_EXPERTS + MOE_GROUPS].set(b_group.astype(F32))
    full = lambda shape: pl.BlockSpec(shape, lambda i: (0,) * len(shape))
    row = lambda w: pl.BlockSpec((tm, w), lambda i: (i, 0))
    has_yb = yb is not None
    if has_yb:
        args = [yT, yb, x2d, wb16[:wa], wb16[wa:]]
        specs = [_chunkT_spec(wa, tm, nseq), row(yb.shape[1]), row(d), full((wa, d)), full((w_out.shape[0] - wa, d))]
    else:
        assert wa == w_out.shape[0]
        args = [yT, x2d, wb16]
        specs = [_chunkT_spec(wa, tm, nseq), row(d), full((wa, d))]
    return pl.pallas_call(
        functools.partial(_out_router_kernel, has_yb=has_yb),
        grid=(n // tm,),
        in_specs=specs + [full((1, d)), full((d, LANES)), full((d, LANES)), full((1, LANES))],
        out_specs=[row(d), pl.BlockSpec((tm * SUBLANES, LANES), lambda i: (i, 0)), row(LANES), row(LANES)],
        out_shape=[jax.ShapeDtypeStruct((n, d), F32), jax.ShapeDtypeStruct((n * SUBLANES, LANES), F32),
                   jax.ShapeDtypeStruct((n, LANES), jnp.int32), jax.ShapeDtypeStruct((n, LANES), F32)],
        compiler_params=_cparams("parallel"),
        name="out_proj_router",
    )(*args, g_ffn[None, :].astype(F32), wrh, wrl, br)


def _moe_dispatch(ids, gates, n):
    bm = MOE_BM
    a = n * MOE_TOPK
    p = a + MOE_EXPERTS * bm
    flat_e = ids.reshape(-1)
    flat_w = gates.reshape(-1)
    order = jnp.argsort(flat_e).astype(jnp.int32)
    experts = jnp.arange(MOE_EXPERTS, dtype=jnp.int32)
    counts = jnp.sum((flat_e[:, None] == experts[None, :]).astype(jnp.int32), axis=0)
    pcounts = (counts + bm - 1) // bm * bm
    starts = jnp.cumsum(counts) - counts
    pends = jnp.cumsum(pcounts)
    pstarts = pends - pcounts
    n_blk = p // bm
    blk_start = jnp.arange(n_blk, dtype=jnp.int32) * bm
    blk_e = jnp.minimum(jnp.sum((blk_start[:, None] >= pends[None, :]).astype(jnp.int32), axis=1),
                        MOE_EXPERTS - 1)
    off = (blk_start - pstarts[blk_e])[:, None] + jnp.arange(bm, dtype=jnp.int32)[None, :]
    valid = (off < counts[blk_e][:, None]).reshape(p)
    sidx = jnp.clip(starts[blk_e][:, None] + off, 0, a - 1).reshape(p)
    assign = order[sidx]
    tok = assign // MOE_TOPK
    slot = assign % MOE_TOPK
    src_tok = jnp.where(valid, tok, 0)
    dump = a + jnp.cumsum(1 - valid.astype(jnp.int32)) - 1
    dst_row = jnp.where(valid, slot * n + tok, dump).astype(jnp.int32)
    row_w = jnp.where(valid, flat_w[assign], 0.0)
    tok_bits = (n - 1).bit_length()
    assert tok_bits + (p - 1).bit_length() <= 32
    packed = lax.bitcast_convert_type(
        src_tok.astype(jnp.uint32) | (dst_row.astype(jnp.uint32) << tok_bits), jnp.int32)
    return blk_e, packed, row_w[:, None], tok_bits


def _moe_kernel(be_ref, idx_ref, h_hbm, w1_ref, w3_ref, w2_ref, rw_ref, y_hbm,
                xbuf, obuf, gsem, ssem, *, tok_bits):
    del be_ref
    bm = MOE_BM
    i = pl.program_id(0)
    nb = pl.num_programs(0)
    slot = i % 2
    tok_mask = (1 << tok_bits) - 1

    tile = lambda r: pl.ds(pl.multiple_of(r * SUBLANES, SUBLANES), SUBLANES)

    def row_loop(body):
        lax.fori_loop(0, bm, lambda r, c: (body(r), c)[1], 0, unroll=MOE_DMA_UNROLL)

    def gather_start(blk, sl):
        def body(r):
            t = idx_ref[blk * bm + r] & tok_mask
            pltpu.make_async_copy(h_hbm.at[tile(t)], xbuf.at[sl, tile(r)], gsem.at[sl]).start()
        row_loop(body)

    def gather_wait(sl):
        pltpu.make_async_copy(h_hbm.at[pl.ds(0, bm * SUBLANES)], xbuf.at[sl], gsem.at[sl]).wait()

    def scatter_start(blk, sl):
        def body(r):
            t = lax.shift_right_logical(idx_ref[blk * bm + r], tok_bits)
            pltpu.make_async_copy(obuf.at[sl, tile(r)], y_hbm.at[tile(t)], ssem.at[sl]).start()
        row_loop(body)

    def scatter_wait(sl):
        pltpu.make_async_copy(obuf.at[sl], y_hbm.at[pl.ds(0, bm * SUBLANES)], ssem.at[sl]).wait()

    @pl.when(i == 0)
    def _():
        gather_start(0, 0)

    @pl.when(i + 1 < nb)
    def _():
        gather_start(i + 1, 1 - slot)

    gather_wait(slot)
    x = _rows_from_tiles(xbuf.at[slot]).astype(BF16)
    h1 = jnp.dot(x, w1_ref[0].astype(BF16), preferred_element_type=F32)
    h3 = jnp.dot(x, w3_ref[0].astype(BF16), preferred_element_type=F32)
    hm = (h1 * _sigmoid(h1) * h3).astype(BF16)
    y = jnp.dot(hm, w2_ref[0].astype(BF16), preferred_element_type=F32) * rw_ref[...]

    @pl.when(i >= 2)
    def _():
        scatter_wait(slot)

    _rows_to_tiles(obuf.at[slot], y)
    scatter_start(i, slot)

    @pl.when(i == nb - 1)
    def _():
        scatter_wait(slot)
        scatter_wait(1 - slot)


def _moe_ffn(h_tiles, ids, gates, w1, w3, w2):
    d = SUBLANES * LANES
    n = h_tiles.shape[0] // SUBLANES
    assert w1.shape[1] == d and h_tiles.shape[1] == LANES
    bm = MOE_BM
    hid = w1.shape[-1]
    blk_e, packed, row_w, tok_bits = _moe_dispatch(ids, gates, n)
    n_blk = blk_e.shape[0]
    rows_out = n * MOE_TOPK + MOE_EXPERTS * bm
    grid_spec = pltpu.PrefetchScalarGridSpec(
        num_scalar_prefetch=2,
        grid=(n_blk,),
        in_specs=[pl.BlockSpec(memory_space=pl.ANY),
                  pl.BlockSpec((1, d, hid), lambda i, be, ix: (be[i], 0, 0)),
                  pl.BlockSpec((1, d, hid), lambda i, be, ix: (be[i], 0, 0)),
                  pl.BlockSpec((1, hid, d), lambda i, be, ix: (be[i], 0, 0)),
                  pl.BlockSpec((bm, 1), lambda i, be, ix: (i, 0))],
        out_specs=pl.BlockSpec(memory_space=pl.ANY),
        scratch_shapes=[pltpu.VMEM((2, bm * SUBLANES, LANES), F32), pltpu.VMEM((2, bm * SUBLANES, LANES), F32),
                        pltpu.SemaphoreType.DMA((2,)), pltpu.SemaphoreType.DMA((2,))],
    )
    return pl.pallas_call(
        functools.partial(_moe_kernel, tok_bits=tok_bits),
        grid_spec=grid_spec,
        out_shape=jax.ShapeDtypeStruct((rows_out * SUBLANES, LANES), F32),
        compiler_params=_cparams("arbitrary"),
        name="moe_ffn",
    )(blk_e, packed, h_tiles, w1, w3, w2, row_w)


def _odd_in_kernel(x_ref, y0_ref, y1_ref, g_ref, w_ref, xo_ref, q_ref, k_ref, v_ref, *, cw):
    x = x_ref[...] + _rows_from_tiles(y0_ref) + _rows_from_tiles(y1_ref)
    xo_ref[...] = x
    h = _rms_rows(x, g_ref[...], EPS)
    z = jnp.dot(h.astype(BF16), w_ref[...], preferred_element_type=F32)
    q_ref[0, 0] = (z[:, :cw] * (C_HEAD_DIM ** -0.5 * LOG2E)).T.astype(BF16)
    k_ref[...] = z[:, cw:2 * cw].astype(BF16)
    v_ref[0, 0] = z[:, 2 * cw:3 * cw].T.astype(BF16)


def _odd_in(x2d, ybuf, g, w_in, seq):
    n, d = x2d.shape
    tm = TM_PROJ
    cw = w_in.shape[1] // 3
    nt = n // tm
    row = lambda w: pl.BlockSpec((tm, w), lambda i: (i, 0))
    full = lambda shape: pl.BlockSpec(shape, lambda i: (0,) * len(shape))
    return pl.pallas_call(
        functools.partial(_odd_in_kernel, cw=cw),
        grid=(nt,),
        in_specs=[row(d), pl.BlockSpec((tm * SUBLANES, LANES), lambda i: (i, 0)),
                  pl.BlockSpec((tm * SUBLANES, LANES), lambda i: (i + nt, 0)), full((1, d)), full(w_in.shape)],
        out_specs=[row(d), _chunkT_spec(cw, tm, seq // tm), row(cw), _chunkT_spec(cw, tm, seq // tm)],
        out_shape=[jax.ShapeDtypeStruct((n, d), F32), jax.ShapeDtypeStruct((n // seq, seq // tm, cw, tm), BF16),
                   jax.ShapeDtypeStruct((n, cw), BF16), jax.ShapeDtypeStruct((n // seq, seq // tm, cw, tm), BF16)],
        compiler_params=_cparams("parallel"),
        name="odd_in_proj",
    )(x2d, ybuf, ybuf, g[None, :].astype(F32), w_in.astype(BF16))


def _t5_bucket(rel):
    nb = REL_BUCKETS // 2
    max_exact = nb // 2
    ret = jnp.where(rel > 0, nb, 0).astype(jnp.int32)
    n = jnp.abs(rel)
    nf = jnp.maximum(n, 1).astype(F32)
    large = max_exact + (jnp.log(nf / max_exact) / math.log(REL_MAX_DIST / max_exact) * (nb - max_exact)).astype(jnp.int32)
    large = jnp.minimum(large, nb - 1)
    return ret + jnp.where(n < max_exact, n, large)


def _diff_kernel(cfar_ref, lam_ref, qT_ref, k_ref, vT_ref, band_ref, gsub_ref, o_ref, s_a, s_b, *, nk, tk):
    head = pl.program_id(1)
    i = pl.program_id(2)
    hd = C_HEAD_DIM
    tq = qT_ref.shape[3]
    dv = vT_ref.shape[2]
    qT = qT_ref[0, 0]
    row_map = lax.broadcasted_iota(jnp.int32, qT.shape, 0) // hd
    qs = [jnp.where(row_map == c, qT, jnp.zeros((), BF16)) for c in range(2)]

    chunk = lambda t: (i + (nk - 1) + t) % nk

    def score_fn(t, c, lead):
        j = chunk(t)
        s = jnp.dot(k_ref[0, pl.ds(pl.multiple_of(j * tk, tk), tk), :], qs[c], preferred_element_type=F32)
        if lead:
            s = s + band_ref[0, jnp.clip(j - i, -2, 2) + 2]
        return s

    def shift_fn(t, lead):
        if lead:
            return None
        return jnp.where(chunk(t) < i, cfar_ref[head, 0], cfar_ref[head, 1])

    o0, o1 = _flash_pipelined(nk, DIFF_LEAD, 2, score_fn, shift_fn, lambda t: vT_ref[0, chunk(t)],
                              s_a, s_b, tq, dv)
    o = o0 - lam_ref[...] * o1
    ms = jnp.mean(o * o, axis=0, keepdims=True)
    o_ref[0, 0] = (o * lax.rsqrt(ms + 1e-5) * gsub_ref[...]).astype(o_ref.dtype)


def _bias_band(rel_bias, t):
    assert t >= REL_MAX_DIST
    rb = rel_bias.astype(F32) * LOG2E
    period = 2 * t
    m = jnp.arange(period, dtype=jnp.int32)
    d3 = jnp.arange(-1, 2, dtype=jnp.int32)[:, None] * t
    rel = jnp.where(m[None, :] < t, d3 - m[None, :], d3 + (period - m[None, :]))
    vec = rb[_t5_bucket(rel)].transpose(2, 0, 1)
    nh = vec.shape[0]
    skew = jnp.tile(vec, (1, 1, t))[:, :, :t * (period - 1)].reshape(nh, 3, t, period - 1)[..., :t]
    far = lambda r: jnp.broadcast_to(rb[_t5_bucket(jnp.int32(r))][:, None, None, None], (nh, 1, t, t))
    return jnp.concatenate([far(-2 * t), skew, far(2 * t)], axis=1)


def _diff_attention(qT, k, vT, lam, lam_init, subln, rel_bias, batch, seq):
    hd = C_HEAD_DIM
    dv = 2 * hd
    tq = qT.shape[3]
    tk = vT.shape[3]
    nk = seq // tk
    assert tk == tq
    k3 = k.reshape(batch, seq, C_HEADS * dv)
    assert nk >= DIFF_LEAD + 2
    band = _bias_band(rel_bias, tk)
    rb = rel_bias.astype(F32) * LOG2E
    cfar = jnp.stack([rb[_t5_bucket(jnp.int32(-2 * tk))], rb[_t5_bucket(jnp.int32(2 * tk))]], axis=1)
    lam_vec = jnp.full((1, tq), lam, F32)
    gsub = (subln.astype(F32) * (1.0 - lam_init))[:, None]
    return pl.pallas_call(
        functools.partial(_diff_kernel, nk=nk, tk=tk),
        grid=(batch, C_HEADS, seq // tq),
        in_specs=[pl.BlockSpec(memory_space=pltpu.SMEM),
                  pl.BlockSpec((1, tq), lambda b, h, i: (0, 0)),
                  pl.BlockSpec((1, 1, dv, tq), lambda b, h, i: (b, i, h, 0)),
                  pl.BlockSpec((1, seq, dv), lambda b, h, i: (b, 0, h)),
                  pl.BlockSpec((1, nk, dv, tk), lambda b, h, i: (b, 0, h, 0)),
                  pl.BlockSpec((1, 5, tk, tq), lambda b, h, i: (h, 0, 0, 0)),
                  pl.BlockSpec((dv, 1), lambda b, h, i: (0, 0))],
        out_specs=pl.BlockSpec((1, 1, dv, tq), lambda b, h, i: (b, i, h, 0)),
        out_shape=jax.ShapeDtypeStruct(qT.shape, BF16),
        scratch_shapes=[pltpu.VMEM((2, tk, tq), F32), pltpu.VMEM((2, tk, tq), F32)],
        compiler_params=_cparams("parallel", "parallel", "parallel"),
        name="diff_attention",
    )(cfar, lam_vec, qT, k3, vT, band, gsub)


def _final_kernel(x_ref, y0_ref, y1_ref, g_ref, o_ref):
    x = x_ref[...] + _rows_from_tiles(y0_ref) + _rows_from_tiles(y1_ref)
    o_ref[...] = _rms_rows(x, g_ref[...], EPS)


def _final(x2d, ybuf, g):
    n, d = x2d.shape
    tm = TM_PROJ
    nt = n // tm
    row = pl.BlockSpec((tm, d), lambda i: (i, 0))
    return pl.pallas_call(
        _final_kernel,
        grid=(nt,),
        in_specs=[row, pl.BlockSpec((tm * SUBLANES, LANES), lambda i: (i, 0)),
                  pl.BlockSpec((tm * SUBLANES, LANES), lambda i: (i + nt, 0)), pl.BlockSpec((1, d), lambda i: (0, 0))],
        out_specs=row,
        out_shape=jax.ShapeDtypeStruct((n, d), F32),
        compiler_params=_cparams("parallel"),
        name="final_norm",
    )(x2d, ybuf, ybuf, g[None, :].astype(F32))


def kernel(x, norm_mix, norm_ffn, norm_final, ev_w_in, ev_q_norm, ev_k_norm, ev_conv_w, ev_conv_b, ev_gn_g, ev_gn_b, ev_w_out, od_w_in, od_lam_q1, od_lam_k1, od_lam_q2, od_lam_k2, od_subln, od_w_out, rel_bias, moe_w_group, moe_b_group, moe_w_expert, moe_b_expert, moe_w1, moe_w3, moe_w2):
    batch, seq, d = x.shape
    n = batch * seq
    x2d = x.reshape(n, d).astype(F32)

    qT, k, vT, glu = _even_in(x2d, norm_mix[0], ev_w_in[0], ev_q_norm[0], ev_k_norm[0], seq)
    yaT = _gqa_attention(qT, k, vT, batch, seq)
    yb = _conformer_conv(glu, ev_conv_w[0], ev_conv_b[0], ev_gn_g[0], ev_gn_b[0], batch, seq)
    x1, h, ids, gates = _out_router(yaT, yb, x2d, ev_w_out[0], norm_ffn[0],
                                    moe_w_group[0], moe_b_group[0], moe_w_expert[0], moe_b_expert[0])
    ybuf = _moe_ffn(h, ids[:, :MOE_TOPK], gates[:, :MOE_TOPK], moe_w1[0], moe_w3[0], moe_w2[0])

    x2, qT, k, vT = _odd_in(x1, ybuf, norm_mix[1], od_w_in[0], seq)
    layer_idx = 1
    lam_init = 0.8 - 0.6 * math.exp(-0.3 * layer_idx)
    lam = (jnp.exp(jnp.sum(od_lam_q1[0].astype(F32) * od_lam_k1[0].astype(F32)))
           - jnp.exp(jnp.sum(od_lam_q2[0].astype(F32) * od_lam_k2[0].astype(F32))) + lam_init)
    oT = _diff_attention(qT, k, vT, lam, lam_init, od_subln[0], rel_bias, batch, seq)
    x3, h, ids, gates = _out_router(oT, None, x2, od_w_out[0], norm_ffn[1],
                                    moe_w_group[1], moe_b_group[1], moe_w_expert[1], moe_b_expert[1])
    ybuf = _moe_ffn(h, ids[:, :MOE_TOPK], gates[:, :MOE_TOPK], moe_w1[1], moe_w3[1], moe_w2[1])
    out = _final(x3, ybuf, norm_final)
    return out.reshape(batch, seq, d)
```

```python
import functools
import math

import jax
import jax.numpy as jnp
from jax import lax
from jax.experimental import pallas as pl
from jax.experimental.pallas import tpu as pltpu

F32 = jnp.float32
BF16 = jnp.bfloat16

GRID_W = 64
HEAD_DIM = 64
EPS = 1e-6
A_HEADS = 8
A_KV_HEADS = 2
A_GROUP = A_HEADS // A_KV_HEADS
ROPE_THETA = 10000.0
ROPE_HALF = HEAD_DIM // 4
B_GROUP_WIDTH = 64
CONV_WIDTH = 31
CONV_HALO = 16
C_HEADS = 8
C_HEAD_DIM = 64
REL_BUCKETS = 32
REL_MAX_DIST = 128
MOE_GROUPS = 4
MOE_PER_GROUP = 8
MOE_EXPERTS = MOE_GROUPS * MOE_PER_GROUP
MOE_TOPK = 2
LOG2E = 1.4426950408889634
NEG_BIG = -1e30
LANES = 128
SUBLANES = 8
MOE_DMA_UNROLL = 8

TM_PROJ = 512
HEADS_PER_STEP_A = 2
DIFF_LEAD = 4
BF16_ROWS = 16
TS_CONV = 256
CONV_SUB = 64
MOE_BM = 128
VMEM_LIMIT = 56 * 1024 * 1024


def _cparams(*sem):
    return pltpu.CompilerParams(dimension_semantics=sem, vmem_limit_bytes=VMEM_LIMIT)


def _split_dot(a, b_bf16):
    hi = a.astype(BF16)
    lo = (a - hi.astype(F32)).astype(BF16)
    return (jnp.dot(hi, b_bf16, preferred_element_type=F32)
            + jnp.dot(lo, b_bf16, preferred_element_type=F32))


def _rms_rows(x, g, eps):
    ms = jnp.mean(x * x, axis=-1, keepdims=True)
    return x * lax.rsqrt(ms + eps) * g


def _sigmoid(x):
    return 1.0 / (1.0 + jnp.exp(-x))


def _chunkT_spec(width, tm, nseq):
    return pl.BlockSpec((1, 1, width, tm), lambda i: (i // nseq, i % nseq, 0, 0))


def _rows_from_tiles(ref):
    rows = ref.shape[0] // SUBLANES
    return jnp.concatenate([ref[pl.ds(s, rows, stride=SUBLANES), :] for s in range(SUBLANES)], axis=1)


def _rows_to_tiles(ref, val):
    rows = val.shape[0]
    for s in range(SUBLANES):
        ref[pl.ds(s, rows, stride=SUBLANES), :] = val[:, s * LANES:(s + 1) * LANES]


def _even_in_kernel(x_ref, g_ref, w_ref, gqk_ref, bd_ref, cos_ref, s1_ref, s2_ref,
                    q_ref, k_ref, v_ref, glu_ref, *, qk_w, q_w, kv_w, b_w):
    h = _rms_rows(x_ref[...], g_ref[...], EPS)
    z = jnp.dot(h.astype(BF16), w_ref[...], preferred_element_type=F32)
    qk = z[:, :qk_w]
    ms = _split_dot(qk * qk, bd_ref[...])
    qkn = qk * lax.rsqrt(ms + EPS) * gqk_ref[...]
    cos, s1, s2 = cos_ref[...], s1_ref[...], s2_ref[...]
    parts = []
    for c in range(qk_w // LANES):
        blk = qkn[:, c * LANES:(c + 1) * LANES]
        up = pltpu.roll(blk, LANES - ROPE_HALF, 1)
        dn = pltpu.roll(blk, ROPE_HALF, 1)
        parts.append(blk * cos + up * s1 + dn * s2)
    rot = jnp.concatenate(parts, axis=1)
    q_ref[0, 0] = rot[:, :q_w].T.astype(BF16)
    k_ref[...] = rot[:, q_w:qk_w].astype(BF16)
    v_ref[0, 0] = z[:, qk_w:qk_w + kv_w].T.astype(BF16)
    a = z[:, qk_w + kv_w:qk_w + kv_w + b_w]
    gate = z[:, qk_w + kv_w + b_w:qk_w + kv_w + 2 * b_w]
    glu_ref[...] = a * _sigmoid(gate)


def _even_in(x2d, g, w_in, q_norm, k_norm, seq):
    n, d = x2d.shape
    q_w, kv_w, b_w = A_HEADS * HEAD_DIM, A_KV_HEADS * HEAD_DIM, d // 2
    qk_w = q_w + kv_w
    tm = TM_PROJ
    gq = jnp.tile(q_norm.astype(F32), A_HEADS) * (HEAD_DIM ** -0.5 * LOG2E)
    gk = jnp.tile(k_norm.astype(F32), A_KV_HEADS)
    gqk = jnp.concatenate([gq, gk])[None, :]
    hid = jnp.arange(qk_w) // HEAD_DIM
    bd = (hid[:, None] == hid[None, :]).astype(BF16) * (1.0 / HEAD_DIM)
    pos = jnp.arange(seq, dtype=jnp.int32)
    row, col = pos // GRID_W, pos % GRID_W
    m = ROPE_HALF
    inv = ROPE_THETA ** (-jnp.arange(m, dtype=F32) / m)
    ang_r = row.astype(F32)[:, None] * inv[None, :]
    ang_c = col.astype(F32)[:, None] * inv[None, :]
    zero = jnp.zeros_like(ang_r)
    cos64 = jnp.concatenate([jnp.cos(ang_r), jnp.cos(ang_r), jnp.cos(ang_c), jnp.cos(ang_c)], axis=1)
    s1_64 = jnp.concatenate([-jnp.sin(ang_r), zero, -jnp.sin(ang_c), zero], axis=1)
    s2_64 = jnp.concatenate([zero, jnp.sin(ang_r), zero, jnp.sin(ang_c)], axis=1)
    rep = LANES // HEAD_DIM
    cos_t, s1_t, s2_t = (jnp.tile(t, (1, rep)) for t in (cos64, s1_64, s2_64))
    nseq = seq // tm
    tab_spec = pl.BlockSpec((tm, LANES), lambda i: (i % nseq, 0))
    full = lambda shape: pl.BlockSpec(shape, lambda i: (0,) * len(shape))
    kern = functools.partial(_even_in_kernel, qk_w=qk_w, q_w=q_w, kv_w=kv_w, b_w=b_w)
    return pl.pallas_call(
        kern,
        grid=(n // tm,),
        in_specs=[pl.BlockSpec((tm, d), lambda i: (i, 0)), full((1, d)), full(w_in.shape),
                  full((1, qk_w)), full((qk_w, qk_w)), tab_spec, tab_spec, tab_spec],
        out_specs=[_chunkT_spec(q_w, tm, nseq), pl.BlockSpec((tm, kv_w), lambda i: (i, 0)),
                   _chunkT_spec(kv_w, tm, nseq), pl.BlockSpec((tm, b_w), lambda i: (i, 0))],
        out_shape=[jax.ShapeDtypeStruct((n // seq, nseq, q_w, tm), BF16), jax.ShapeDtypeStruct((n, kv_w), BF16),
                   jax.ShapeDtypeStruct((n // seq, nseq, kv_w, tm), BF16), jax.ShapeDtypeStruct((n, b_w), F32)],
        compiler_params=_cparams("parallel"),
        name="even_in_proj",
    )(x2d, g[None, :].astype(F32), w_in.astype(BF16), gqk, bd, cos_t, s1_t, s2_t)


def _flash_pipelined(nk, n_lead, n_maps, score_fn, shift_fn, value_fn, s_a, s_b, tq, dv):
    assert nk % 2 == 0 and n_lead % 2 == 0 and nk - n_lead >= 2
    ones = jnp.ones((BF16_ROWS, s_a.shape[1]), BF16)

    def scores(t, s_ref, lead):
        shift = shift_fn(t, lead)
        cms = []
        for c in range(n_maps):
            s = score_fn(t, c, lead)
            s_ref[c] = s
            cm = jnp.max(s, axis=0, keepdims=True)
            cms.append(cm if shift is None else cm + shift)
        return tuple(cms)

    def softmax(t, s_ref, cms, state, lead):
        shift = shift_fn(t, lead)
        vals = jnp.concatenate([value_fn(t), ones], axis=0)
        new = []
        for c in range(n_maps):
            m, acc = state[c]
            m_new = jnp.maximum(m, cms[c])
            alpha = jnp.exp2(m - m_new)
            p = jnp.exp2(s_ref[c] - (m_new if shift is None else m_new - shift))
            acc = alpha * acc + jnp.dot(vals, p.astype(BF16), preferred_element_type=F32)
            new.append((m_new, acc))
        return tuple(new)

    state = tuple((jnp.full((1, tq), NEG_BIG, F32), jnp.zeros((dv + BF16_ROWS, tq), F32)) for _ in range(n_maps))
    bufs = (s_a, s_b)
    cms = scores(0, s_a, n_lead > 0)
    for t in range(n_lead):
        nxt = scores(t + 1, bufs[(t + 1) % 2], t + 1 < n_lead)
        state = softmax(t, bufs[t % 2], cms, state, True)
        cms = nxt

    def pair(u, carry):
        cms_a, state = carry
        t = n_lead + 2 * u
        cms_b = scores(t + 1, s_b, False)
        state = softmax(t, s_a, cms_a, state, False)
        cms_a = scores(t + 2, s_a, False)
        state = softmax(t + 1, s_b, cms_b, state, False)
        return cms_a, state

    cms_a, state = lax.fori_loop(0, (nk - n_lead) // 2 - 1, pair, (cms, state))
    cms_b = scores(nk - 1, s_b, False)
    state = softmax(nk - 2, s_a, cms_a, state, False)
    state = softmax(nk - 1, s_b, cms_b, state, False)
    return tuple(acc[:dv] / acc[dv:dv + 1] for _, acc in state)


def _gqa_kernel(qT_ref, k_ref, vT_ref, o_ref, s_a, s_b, *, nk, tk, nh):
    hd = HEAD_DIM
    tq = qT_ref.shape[3]
    kv = pl.program_id(1) // (A_GROUP // nh)
    row_head = lax.broadcasted_iota(jnp.int32, (A_KV_HEADS * hd, tq), 0) // hd
    qs = []
    for h in range(nh):
        qh = qT_ref[0, 0, h * hd:(h + 1) * hd, :]
        qs.append(jnp.where(row_head == kv, jnp.concatenate([qh] * A_KV_HEADS, axis=0), jnp.zeros((), BF16)))

    def score_fn(t, c, lead):
        kc = k_ref[0, pl.ds(pl.multiple_of(t * tk, tk), tk), :]
        return jnp.dot(kc, qs[c], preferred_element_type=F32)

    res = _flash_pipelined(nk, 0, nh, score_fn, lambda t, lead: None, lambda t: vT_ref[0, t], s_a, s_b, tq, hd)
    for h, o in enumerate(res):
        o_ref[0, 0, h * hd:(h + 1) * hd, :] = o.astype(o_ref.dtype)


def _gqa_attention(qT, k, vT, batch, seq):
    hd, nh = HEAD_DIM, HEADS_PER_STEP_A
    tq = qT.shape[3]
    tk = vT.shape[3]
    nk = seq // tk
    hsteps = A_HEADS // nh
    per_kv = A_GROUP // nh
    k3 = k.reshape(batch, seq, A_KV_HEADS * hd)
    return pl.pallas_call(
        functools.partial(_gqa_kernel, nk=nk, tk=tk, nh=nh),
        grid=(batch, hsteps, seq // tq),
        in_specs=[pl.BlockSpec((1, 1, nh * hd, tq), lambda b, h, i: (b, i, h, 0)),
                  pl.BlockSpec((1, seq, A_KV_HEADS * hd), lambda b, h, i: (b, 0, 0)),
                  pl.BlockSpec((1, nk, hd, tk), lambda b, h, i: (b, 0, h // per_kv, 0))],
        out_specs=pl.BlockSpec((1, 1, nh * hd, tq), lambda b, h, i: (b, i, h, 0)),
        out_shape=jax.ShapeDtypeStruct(qT.shape, BF16),
        scratch_shapes=[pltpu.VMEM((nh, tk, tq), F32), pltpu.VMEM((nh, tk, tq), F32)],
        compiler_params=_cparams("parallel", "parallel", "parallel"),
        name="gqa_attention",
    )(qT, k3, vT)


def _conv_kernel(prev_ref, cur_ref, next_ref, w_ref, b_ref, g_ref, beta_ref, bd_ref, o_ref, xe_ref, *, ts):
    i = pl.program_id(1)
    last = pl.num_programs(1) - 1
    halo = CONV_HALO
    xe_ref[0:halo, :] = jnp.where(i > 0, prev_ref[0], 0.0)
    xe_ref[halo:halo + ts, :] = cur_ref[0]
    xe_ref[halo + ts:halo + ts + halo, :] = jnp.where(i < last, next_ref[0], 0.0)
    w = w_ref[...]
    first = halo - CONV_WIDTH // 2
    for r0 in range(0, ts, CONV_SUB):
        acc = jnp.zeros((CONV_SUB, w.shape[1]), F32) + b_ref[...]
        for j in range(CONV_WIDTH):
            acc = acc + w[j:j + 1, :] * xe_ref[r0 + first + j:r0 + first + j + CONV_SUB, :]
        mu = _split_dot(acc, bd_ref[...])
        cen = acc - mu
        var = _split_dot(cen * cen, bd_ref[...])
        hn = cen * lax.rsqrt(var + 1e-5) * g_ref[...] + beta_ref[...]
        o_ref[0, r0:r0 + CONV_SUB, :] = (hn * _sigmoid(hn)).astype(o_ref.dtype)


def _conformer_conv(glu, conv_w, conv_b, gn_g, gn_b, batch, seq):
    c = glu.shape[1]
    ts, halo = TS_CONV, CONV_HALO
    x3 = glu.reshape(batch, seq, c)
    hpb = ts // halo
    nh = seq // halo
    gid = jnp.arange(c) // B_GROUP_WIDTH
    bd = (gid[:, None] == gid[None, :]).astype(BF16) * (1.0 / B_GROUP_WIDTH)
    wpad = jnp.zeros((32, c), F32).at[:CONV_WIDTH].set(conv_w.astype(F32))
    vec = lambda a: a[None, :].astype(F32)
    full = lambda shape: pl.BlockSpec(shape, lambda b, i: (0,) * len(shape))
    out = pl.pallas_call(
        functools.partial(_conv_kernel, ts=ts),
        grid=(batch, seq // ts),
        in_specs=[pl.BlockSpec((1, halo, c), lambda b, i: (b, jnp.maximum(i * hpb - 1, 0), 0)),
                  pl.BlockSpec((1, ts, c), lambda b, i: (b, i, 0)),
                  pl.BlockSpec((1, halo, c), lambda b, i: (b, jnp.minimum((i + 1) * hpb, nh - 1), 0)),
                  full((32, c)), full((1, c)), full((1, c)), full((1, c)), full((c, c))],
        out_specs=pl.BlockSpec((1, ts, c), lambda b, i: (b, i, 0)),
        out_shape=jax.ShapeDtypeStruct((batch, seq, c), BF16),
        scratch_shapes=[pltpu.VMEM((ts + 2 * halo, c), F32)],
        compiler_params=_cparams("parallel", "parallel"),
        name="conformer_conv",
    )(x3, x3, x3, wpad, vec(conv_b), vec(gn_g), vec(gn_b), bd)
    return out.reshape(batch * seq, c)


def _route(logits):
    lane = lax.broadcasted_iota(jnp.int32, logits.shape, 1)
    big = jnp.int32(1 << 20)
    ninf = jnp.float32(-jnp.inf)
    gmask = (lane >= MOE_EXPERTS) & (lane < MOE_EXPERTS + MOE_GROUPS)
    lg = jnp.where(gmask, logits, ninf)
    mg = jnp.max(lg, axis=-1, keepdims=True)
    sg = jnp.sum(jnp.exp(lg - mg), axis=-1, keepdims=True)
    g_val = 1.0 / sg
    g_idx = jnp.min(jnp.where(lg == mg, lane - MOE_EXPERTS, big), axis=-1, keepdims=True)
    emask = (lane < MOE_EXPERTS) & ((lane // MOE_PER_GROUP) == g_idx)
    le = jnp.where(emask, logits, ninf)
    m1 = jnp.max(le, axis=-1, keepdims=True)
    se = jnp.sum(jnp.exp(le - m1), axis=-1, keepdims=True)
    i1 = jnp.min(jnp.where(le == m1, lane, big), axis=-1, keepdims=True)
    le2 = jnp.where(lane == i1, ninf, le)
    m2 = jnp.max(le2, axis=-1, keepdims=True)
    i2 = jnp.min(jnp.where(le2 == m2, lane, big), axis=-1, keepdims=True)
    e1 = 1.0 / se
    e2 = jnp.exp(m2 - m1) / se
    norm = g_val / (e1 + e2)
    ids = jnp.where(lane == 0, i1, i2).astype(F32)
    gates = jnp.where(lane == MOE_TOPK, e1 * norm, e2 * norm)
    return jnp.where(lane < MOE_TOPK, ids, gates)


def _out_router_kernel(*refs, has_yb):
    if has_yb:
        yT_ref, yb_ref, x_ref, wa_ref, wb_ref = refs[:5]
        rest = refs[5:]
    else:
        yT_ref, x_ref, wa_ref = refs[:3]
        rest = refs[3:]
    g_ref, wrh_ref, wrl_ref, br_ref, x1_ref, h_ref, route_ref = rest
    ya = yT_ref[0, 0].astype(F32).T.astype(BF16)
    x1 = x_ref[...] + jnp.dot(ya, wa_ref[...], preferred_element_type=F32)
    if has_yb:
        x1 = x1 + jnp.dot(yb_ref[...], wb_ref[...], preferred_element_type=F32)
    x1_ref[...] = x1
    h = _rms_rows(x1, g_ref[...], EPS)
    _rows_to_tiles(h_ref, h)
    hi = h.astype(BF16)
    lo = (h - hi.astype(F32)).astype(BF16)
    logits = (jnp.dot(hi, wrh_ref[...], preferred_element_type=F32)
              + jnp.dot(hi, wrl_ref[...], preferred_element_type=F32)
              + jnp.dot(lo, wrh_ref[...], preferred_element_type=F32) + br_ref[...])
    route_ref[...] = _route(logits).T[:SUBLANES]


def _out_router(yT, yb, x2d, w_out, g_ffn, w_group, b_group, w_expert, b_expert):
    n, d = x2d.shape
    tm = yT.shape[3]
    nseq = yT.shape[1]
    wa = yT.shape[2]
    wb16 = w_out.astype(BF16)
    wr = jnp.zeros((d, LANES), F32).at[:, :MOE_EXPERTS].set(w_expert.astype(F32))
    wr = wr.at[:, MOE_EXPERTS:MOE_EXPERTS + MOE_GROUPS].set(w_group.astype(F32))
    wrh = wr.astype(BF16)
    wrl = (wr - wrh.astype(F32)).astype(BF16)
    br = jnp.zeros((1, LANES), F32).at[0, :MOE_EXPERTS].set(b_expert.astype(F32))
    br = br.at[0, MOE_EXPERTS:MOE_EXPERTS + MOE_GROUPS].set(b_group.astype(F32))
    full = lambda shape: pl.BlockSpec(shape, lambda i: (0,) * len(shape))
    row = lambda w: pl.BlockSpec((tm, w), lambda i: (i, 0))
    has_yb = yb is not None
    if has_yb:
        args = [yT, yb, x2d, wb16[:wa], wb16[wa:]]
        specs = [_chunkT_spec(wa, tm, nseq), row(yb.shape[1]), row(d), full((wa, d)), full((w_out.shape[0] - wa, d))]
    else:
        assert wa == w_out.shape[0]
        args = [yT, x2d, wb16]
        specs = [_chunkT_spec(wa, tm, nseq), row(d), full((wa, d))]
    return pl.pallas_call(
        functools.partial(_out_router_kernel, has_yb=has_yb),
        grid=(n // tm,),
        in_specs=specs + [full((1, d)), full((d, LANES)), full((d, LANES)), full((1, LANES))],
        out_specs=[row(d), pl.BlockSpec((tm * SUBLANES, LANES), lambda i: (i, 0)),
                   pl.BlockSpec((SUBLANES, tm), lambda i: (0, i))],
        out_shape=[jax.ShapeDtypeStruct((n, d), F32), jax.ShapeDtypeStruct((n * SUBLANES, LANES), F32),
                   jax.ShapeDtypeStruct((SUBLANES, n), F32)],
        compiler_params=_cparams("parallel"),
        name="out_proj_router",
    )(*args, g_ffn[None, :].astype(F32), wrh, wrl, br)


def _moe_dispatch(route, n):
    ids = route[:MOE_TOPK].astype(jnp.int32)
    gates = route[MOE_TOPK:2 * MOE_TOPK]
    bm = MOE_BM
    a = n * MOE_TOPK
    p = a + MOE_EXPERTS * bm
    flat_e = ids.reshape(-1)
    flat_w = gates.reshape(-1)
    order = jnp.argsort(flat_e).astype(jnp.int32)
    experts = jnp.arange(MOE_EXPERTS, dtype=jnp.int32)
    counts = jnp.sum((flat_e[:, None] == experts[None, :]).astype(jnp.int32), axis=0)
    pcounts = (counts + bm - 1) // bm * bm
    starts = jnp.cumsum(counts) - counts
    pends = jnp.cumsum(pcounts)
    pstarts = pends - pcounts
    n_blk = p // bm
    blk_start = jnp.arange(n_blk, dtype=jnp.int32) * bm
    blk_e = jnp.minimum(jnp.sum((blk_start[:, None] >= pends[None, :]).astype(jnp.int32), axis=1),
                        MOE_EXPERTS - 1)
    off = (blk_start - pstarts[blk_e])[:, None] + jnp.arange(bm, dtype=jnp.int32)[None, :]
    valid = (off < counts[blk_e][:, None]).reshape(p)
    sidx = jnp.clip(starts[blk_e][:, None] + off, 0, a - 1).reshape(p)
    assign = order[sidx]
    tok = assign % n
    slot = assign // n
    src_tok = jnp.where(valid, tok, 0)
    dump = a + jnp.cumsum(1 - valid.astype(jnp.int32)) - 1
    dst_row = jnp.where(valid, slot * n + tok, dump).astype(jnp.int32)
    row_w = jnp.where(valid, flat_w[assign], 0.0)
    tok_bits = (n - 1).bit_length()
    src_tok = jnp.concatenate([jnp.zeros((bm,), jnp.int32), src_tok])
    dst_row = jnp.concatenate([p + jnp.arange(bm, dtype=jnp.int32), dst_row])
    assert tok_bits + (p + 2 * bm - 1).bit_length() <= 32
    packed = lax.bitcast_convert_type(
        src_tok.astype(jnp.uint32) | (dst_row.astype(jnp.uint32) << tok_bits), jnp.int32)
    return blk_e, packed, row_w[:, None], tok_bits


def _moe_kernel(be_ref, idx_ref, h_hbm, w1_ref, w3_ref, w2_ref, rw_ref, y_hbm,
                xbuf, obuf, gsem, ssem, *, tok_bits, spare_row):
    del be_ref
    bm = MOE_BM
    i = pl.program_id(0)
    nb = pl.num_programs(0)
    slot = i % 2
    tok_mask = (1 << tok_bits) - 1

    tile = lambda r: pl.ds(pl.multiple_of(r * SUBLANES, SUBLANES), SUBLANES)

    def gather_row(blk, sl, r):
        t = idx_ref[(blk + 1) * bm + r] & tok_mask
        pltpu.make_async_copy(h_hbm.at[tile(t)], xbuf.at[sl, tile(r)], gsem.at[sl]).start()

    def scatter_row(blk, sl, r):
        t = lax.shift_right_logical(idx_ref[(blk + 1) * bm + r], tok_bits)
        pltpu.make_async_copy(obuf.at[sl, tile(r)], y_hbm.at[tile(t)], ssem.at[sl]).start()

    def row_loop(body):
        lax.fori_loop(0, bm, lambda r, c: (body(r), c)[1], 0, unroll=MOE_DMA_UNROLL)

    def gather_wait(sl):
        pltpu.make_async_copy(h_hbm.at[pl.ds(0, bm * SUBLANES)], xbuf.at[sl], gsem.at[sl]).wait()

    def scatter_wait(sl):
        pltpu.make_async_copy(obuf.at[sl], y_hbm.at[pl.ds(0, bm * SUBLANES)], ssem.at[sl]).wait()

    @pl.when(i == 0)
    def _():
        obuf[...] = jnp.zeros(obuf.shape, obuf.dtype)
        row_loop(lambda r: gather_row(0, 0, r))
        row_loop(lambda r: pltpu.make_async_copy(obuf.at[0, tile(r)], y_hbm.at[tile(spare_row + r)],
                                                 ssem.at[0]).start())

    gather_wait(slot)
    nxt = jnp.minimum(i + 1, nb - 1)
    for r in range(bm):
        gather_row(nxt, 1 - slot, r)
        scatter_row(i - 1, 1 - slot, r)
    x = _rows_from_tiles(xbuf.at[slot]).astype(BF16)
    h1 = jnp.dot(x, w1_ref[0].astype(BF16), preferred_element_type=F32)
    h3 = jnp.dot(x, w3_ref[0].astype(BF16), preferred_element_type=F32)
    hm = (h1 * _sigmoid(h1) * h3).astype(BF16)
    y = jnp.dot(hm, w2_ref[0].astype(BF16), preferred_element_type=F32) * rw_ref[...]
    scatter_wait(slot)
    _rows_to_tiles(obuf.at[slot], y)

    @pl.when(i == nb - 1)
    def _():
        row_loop(lambda r: scatter_row(i, slot, r))
        scatter_wait(slot)
        scatter_wait(1 - slot)
        gather_wait(1 - slot)


def _moe_ffn(h_tiles, route, w1, w3, w2):
    d = SUBLANES * LANES
    n = h_tiles.shape[0] // SUBLANES
    assert w1.shape[1] == d and h_tiles.shape[1] == LANES
    bm = MOE_BM
    hid = w1.shape[-1]
    blk_e, packed, row_w, tok_bits = _moe_dispatch(route, n)
    n_blk = blk_e.shape[0]
    rows_real = n * MOE_TOPK + MOE_EXPERTS * bm
    rows_out = rows_real + 2 * bm
    grid_spec = pltpu.PrefetchScalarGridSpec(
        num_scalar_prefetch=2,
        grid=(n_blk,),
        in_specs=[pl.BlockSpec(memory_space=pl.ANY),
                  pl.BlockSpec((1, d, hid), lambda i, be, ix: (be[i], 0, 0)),
                  pl.BlockSpec((1, d, hid), lambda i, be, ix: (be[i], 0, 0)),
                  pl.BlockSpec((1, hid, d), lambda i, be, ix: (be[i], 0, 0)),
                  pl.BlockSpec((bm, 1), lambda i, be, ix: (i, 0))],
        out_specs=pl.BlockSpec(memory_space=pl.ANY),
        scratch_shapes=[pltpu.VMEM((2, bm * SUBLANES, LANES), F32), pltpu.VMEM((2, bm * SUBLANES, LANES), F32),
                        pltpu.SemaphoreType.DMA((2,)), pltpu.SemaphoreType.DMA((2,))],
    )
    return pl.pallas_call(
        functools.partial(_moe_kernel, tok_bits=tok_bits, spare_row=rows_real + bm),
        grid_spec=grid_spec,
        out_shape=jax.ShapeDtypeStruct((rows_out * SUBLANES, LANES), F32),
        compiler_params=_cparams("arbitrary"),
        name="moe_ffn",
    )(blk_e, packed, h_tiles, w1, w3, w2, row_w)


def _odd_in_kernel(x_ref, y0_ref, y1_ref, g_ref, w_ref, xo_ref, q_ref, k_ref, v_ref, *, cw):
    x = x_ref[...] + _rows_from_tiles(y0_ref) + _rows_from_tiles(y1_ref)
    xo_ref[...] = x
    h = _rms_rows(x, g_ref[...], EPS)
    z = jnp.dot(h.astype(BF16), w_ref[...], preferred_element_type=F32)
    q_ref[0, 0] = (z[:, :cw] * (C_HEAD_DIM ** -0.5 * LOG2E)).T.astype(BF16)
    k_ref[...] = z[:, cw:2 * cw].astype(BF16)
    v_ref[0, 0] = z[:, 2 * cw:3 * cw].T.astype(BF16)


def _odd_in(x2d, ybuf, g, w_in, seq):
    n, d = x2d.shape
    tm = TM_PROJ
    cw = w_in.shape[1] // 3
    nt = n // tm
    row = lambda w: pl.BlockSpec((tm, w), lambda i: (i, 0))
    full = lambda shape: pl.BlockSpec(shape, lambda i: (0,) * len(shape))
    return pl.pallas_call(
        functools.partial(_odd_in_kernel, cw=cw),
        grid=(nt,),
        in_specs=[row(d), pl.BlockSpec((tm * SUBLANES, LANES), lambda i: (i, 0)),
                  pl.BlockSpec((tm * SUBLANES, LANES), lambda i: (i + nt, 0)), full((1, d)), full(w_in.shape)],
        out_specs=[row(d), _chunkT_spec(cw, tm, seq // tm), row(cw), _chunkT_spec(cw, tm, seq // tm)],
        out_shape=[jax.ShapeDtypeStruct((n, d), F32), jax.ShapeDtypeStruct((n // seq, seq // tm, cw, tm), BF16),
                   jax.ShapeDtypeStruct((n, cw), BF16), jax.ShapeDtypeStruct((n // seq, seq // tm, cw, tm), BF16)],
        compiler_params=_cparams("parallel"),
        name="odd_in_proj",
    )(x2d, ybuf, ybuf, g[None, :].astype(F32), w_in.astype(BF16))


def _t5_bucket(rel):
    nb = REL_BUCKETS // 2
    max_exact = nb // 2
    ret = jnp.where(rel > 0, nb, 0).astype(jnp.int32)
    n = jnp.abs(rel)
    nf = jnp.maximum(n, 1).astype(F32)
    large = max_exact + (jnp.log(nf / max_exact) / math.log(REL_MAX_DIST / max_exact) * (nb - max_exact)).astype(jnp.int32)
    large = jnp.minimum(large, nb - 1)
    return ret + jnp.where(n < max_exact, n, large)


def _diff_kernel(cfar_ref, lam_ref, qT_ref, k_ref, vT_ref, band_ref, gsub_ref, o_ref, s_a, s_b, *, nk, tk):
    head = pl.program_id(1)
    i = pl.program_id(2)
    hd = C_HEAD_DIM
    tq = qT_ref.shape[3]
    dv = vT_ref.shape[2]
    qT = qT_ref[0, 0]
    row_map = lax.broadcasted_iota(jnp.int32, qT.shape, 0) // hd
    qs = [jnp.where(row_map == c, qT, jnp.zeros((), BF16)) for c in range(2)]

    chunk = lambda t: (i + (nk - 1) + t) % nk

    def score_fn(t, c, lead):
        j = chunk(t)
        s = jnp.dot(k_ref[0, pl.ds(pl.multiple_of(j * tk, tk), tk), :], qs[c], preferred_element_type=F32)
        if lead:
            s = s + band_ref[0, jnp.clip(j - i, -2, 2) + 2]
        return s

    def shift_fn(t, lead):
        if lead:
            return None
        return jnp.where(chunk(t) < i, cfar_ref[head, 0], cfar_ref[head, 1])

    o0, o1 = _flash_pipelined(nk, DIFF_LEAD, 2, score_fn, shift_fn, lambda t: vT_ref[0, chunk(t)],
                              s_a, s_b, tq, dv)
    o = o0 - lam_ref[...] * o1
    ms = jnp.mean(o * o, axis=0, keepdims=True)
    o_ref[0, 0] = (o * lax.rsqrt(ms + 1e-5) * gsub_ref[...]).astype(o_ref.dtype)


def _bias_band(rel_bias, t):
    assert t >= REL_MAX_DIST
    rb = rel_bias.astype(F32) * LOG2E
    period = 2 * t
    m = jnp.arange(period, dtype=jnp.int32)
    d3 = jnp.arange(-1, 2, dtype=jnp.int32)[:, None] * t
    rel = jnp.where(m[None, :] < t, d3 - m[None, :], d3 + (period - m[None, :]))
    vec = rb[_t5_bucket(rel)].transpose(2, 0, 1)
    nh = vec.shape[0]
    skew = jnp.tile(vec, (1, 1, t))[:, :, :t * (period - 1)].reshape(nh, 3, t, period - 1)[..., :t]
    far = lambda r: jnp.broadcast_to(rb[_t5_bucket(jnp.int32(r))][:, None, None, None], (nh, 1, t, t))
    return jnp.concatenate([far(-2 * t), skew, far(2 * t)], axis=1)


def _diff_attention(qT, k, vT, lam, lam_init, subln, rel_bias, batch, seq):
    hd = C_HEAD_DIM
    dv = 2 * hd
    tq = qT.shape[3]
    tk = vT.shape[3]
    nk = seq // tk
    assert tk == tq
    k3 = k.reshape(batch, seq, C_HEADS * dv)
    assert nk >= DIFF_LEAD + 2
    band = _bias_band(rel_bias, tk)
    rb = rel_bias.astype(F32) * LOG2E
    cfar = jnp.stack([rb[_t5_bucket(jnp.int32(-2 * tk))], rb[_t5_bucket(jnp.int32(2 * tk))]], axis=1)
    lam_vec = jnp.full((1, tq), lam, F32)
    gsub = (subln.astype(F32) * (1.0 - lam_init))[:, None]
    return pl.pallas_call(
        functools.partial(_diff_kernel, nk=nk, tk=tk),
        grid=(batch, C_HEADS, seq // tq),
        in_specs=[pl.BlockSpec(memory_space=pltpu.SMEM),
                  pl.BlockSpec((1, tq), lambda b, h, i: (0, 0)),
                  pl.BlockSpec((1, 1, dv, tq), lambda b, h, i: (b, i, h, 0)),
                  pl.BlockSpec((1, seq, dv), lambda b, h, i: (b, 0, h)),
                  pl.BlockSpec((1, nk, dv, tk), lambda b, h, i: (b, 0, h, 0)),
                  pl.BlockSpec((1, 5, tk, tq), lambda b, h, i: (h, 0, 0, 0)),
                  pl.BlockSpec((dv, 1), lambda b, h, i: (0, 0))],
        out_specs=pl.BlockSpec((1, 1, dv, tq), lambda b, h, i: (b, i, h, 0)),
        out_shape=jax.ShapeDtypeStruct(qT.shape, BF16),
        scratch_shapes=[pltpu.VMEM((2, tk, tq), F32), pltpu.VMEM((2, tk, tq), F32)],
        compiler_params=_cparams("parallel", "parallel", "parallel"),
        name="diff_attention",
    )(cfar, lam_vec, qT, k3, vT, band, gsub)


def _final_kernel(x_ref, y0_ref, y1_ref, g_ref, o_ref):
    x = x_ref[...] + _rows_from_tiles(y0_ref) + _rows_from_tiles(y1_ref)
    o_ref[...] = _rms_rows(x, g_ref[...], EPS)


def _final(x2d, ybuf, g):
    n, d = x2d.shape
    tm = TM_PROJ
    nt = n // tm
    row = pl.BlockSpec((tm, d), lambda i: (i, 0))
    return pl.pallas_call(
        _final_kernel,
        grid=(nt,),
        in_specs=[row, pl.BlockSpec((tm * SUBLANES, LANES), lambda i: (i, 0)),
                  pl.BlockSpec((tm * SUBLANES, LANES), lambda i: (i + nt, 0)), pl.BlockSpec((1, d), lambda i: (0, 0))],
        out_specs=row,
        out_shape=jax.ShapeDtypeStruct((n, d), F32),
        compiler_params=_cparams("parallel"),
        name="final_norm",
    )(x2d, ybuf, ybuf, g[None, :].astype(F32))


def kernel(x, norm_mix, norm_ffn, norm_final, ev_w_in, ev_q_norm, ev_k_norm, ev_conv_w, ev_conv_b, ev_gn_g, ev_gn_b, ev_w_out, od_w_in, od_lam_q1, od_lam_k1, od_lam_q2, od_lam_k2, od_subln, od_w_out, rel_bias, moe_w_group, moe_b_group, moe_w_expert, moe_b_expert, moe_w1, moe_w3, moe_w2):
    batch, seq, d = x.shape
    n = batch * seq
    x2d = x.reshape(n, d).astype(F32)

    qT, k, vT, glu = _even_in(x2d, norm_mix[0], ev_w_in[0], ev_q_norm[0], ev_k_norm[0], seq)
    yaT = _gqa_attention(qT, k, vT, batch, seq)
    yb = _conformer_conv(glu, ev_conv_w[0], ev_conv_b[0], ev_gn_g[0], ev_gn_b[0], batch, seq)
    x1, h, route = _out_router(yaT, yb, x2d, ev_w_out[0], norm_ffn[0],
                                    moe_w_group[0], moe_b_group[0], moe_w_expert[0], moe_b_expert[0])
    ybuf = _moe_ffn(h, route, moe_w1[0], moe_w3[0], moe_w2[0])

    x2, qT, k, vT = _odd_in(x1, ybuf, norm_mix[1], od_w_in[0], seq)
    layer_idx = 1
    lam_init = 0.8 - 0.6 * math.exp(-0.3 * layer_idx)
    lam = (jnp.exp(jnp.sum(od_lam_q1[0].astype(F32) * od_lam_k1[0].astype(F32)))
           - jnp.exp(jnp.sum(od_lam_q2[0].astype(F32) * od_lam_k2[0].astype(F32))) + lam_init)
    oT = _diff_attention(qT, k, vT, lam, lam_init, od_subln[0], rel_bias, batch, seq)
    x3, h, route = _out_router(oT, None, x2, od_w_out[0], norm_ffn[1],
                                    moe_w_group[1], moe_b_group[1], moe_w_expert[1], moe_b_expert[1])
    ybuf = _moe_ffn(h, route, moe_w1[1], moe_w3[1], moe_w2[1])
    out = _final(x3, ybuf, norm_final)
    return out.reshape(batch, seq, d)
```

```python
import functools
import math

import jax
import jax.numpy as jnp
from jax import lax
from jax.experimental import pallas as pl
from jax.experimental.pallas import tpu as pltpu

F32 = jnp.float32
BF16 = jnp.bfloat16

GRID_W = 64
HEAD_DIM = 64
EPS = 1e-6
A_HEADS = 8
A_KV_HEADS = 2
A_GROUP = A_HEADS // A_KV_HEADS
ROPE_THETA = 10000.0
ROPE_HALF = HEAD_DIM // 4
B_GROUP_WIDTH = 64
CONV_WIDTH = 31
CONV_HALO = 16
C_HEADS = 8
C_HEAD_DIM = 64
REL_BUCKETS = 32
REL_MAX_DIST = 128
MOE_GROUPS = 4
MOE_PER_GROUP = 8
MOE_EXPERTS = MOE_GROUPS * MOE_PER_GROUP
MOE_TOPK = 2
LOG2E = 1.4426950408889634
NEG_BIG = -1e30
LANES = 128
SUBLANES = 8
MOE_DMA_UNROLL = 8

TM_PROJ = 512
HEADS_PER_STEP_A = 2
DIFF_LEAD = 4
BF16_ROWS = 16
TS_CONV = 256
CONV_SUB = 64
MOE_BM = 128
VMEM_LIMIT = 56 * 1024 * 1024


def _cparams(*sem):
    return pltpu.CompilerParams(dimension_semantics=sem, vmem_limit_bytes=VMEM_LIMIT)


def _split_dot(a, b_bf16):
    hi = a.astype(BF16)
    lo = (a - hi.astype(F32)).astype(BF16)
    return (jnp.dot(hi, b_bf16, preferred_element_type=F32)
            + jnp.dot(lo, b_bf16, preferred_element_type=F32))


def _rms_rows(x, g, eps):
    ms = jnp.mean(x * x, axis=-1, keepdims=True)
    return x * lax.rsqrt(ms + eps) * g


def _sigmoid(x):
    return 1.0 / (1.0 + jnp.exp(-x))


def _chunkT_spec(width, tm, nseq):
    return pl.BlockSpec((1, 1, width, tm), lambda i: (i // nseq, i % nseq, 0, 0))


def _rows_from_tiles(ref):
    rows = ref.shape[0] // SUBLANES
    return jnp.concatenate([ref[pl.ds(s, rows, stride=SUBLANES), :] for s in range(SUBLANES)], axis=1)


def _rows_to_tiles(ref, val):
    rows = val.shape[0]
    for s in range(SUBLANES):
        ref[pl.ds(s, rows, stride=SUBLANES), :] = val[:, s * LANES:(s + 1) * LANES]


def _even_in_kernel(x_ref, g_ref, w_ref, gqk_ref, bd_ref, cos_ref, s1_ref, s2_ref,
                    q_ref, k_ref, v_ref, glu_ref, *, qk_w, q_w, kv_w, b_w):
    h = _rms_rows(x_ref[...], g_ref[...], EPS)
    z = jnp.dot(h.astype(BF16), w_ref[...], preferred_element_type=F32)
    qk = z[:, :qk_w]
    ms = _split_dot(qk * qk, bd_ref[...])
    qkn = qk * lax.rsqrt(ms + EPS) * gqk_ref[...]
    cos, s1, s2 = cos_ref[...], s1_ref[...], s2_ref[...]
    parts = []
    for c in range(qk_w // LANES):
        blk = qkn[:, c * LANES:(c + 1) * LANES]
        up = pltpu.roll(blk, LANES - ROPE_HALF, 1)
        dn = pltpu.roll(blk, ROPE_HALF, 1)
        parts.append(blk * cos + up * s1 + dn * s2)
    rot = jnp.concatenate(parts, axis=1)
    q_ref[0, 0] = rot[:, :q_w].T.astype(BF16)
    k_ref[...] = rot[:, q_w:qk_w].astype(BF16)
    v_ref[0, 0] = z[:, qk_w:qk_w + kv_w].T.astype(BF16)
    a = z[:, qk_w + kv_w:qk_w + kv_w + b_w]
    gate = z[:, qk_w + kv_w + b_w:qk_w + kv_w + 2 * b_w]
    glu_ref[...] = a * _sigmoid(gate)


def _even_in(x2d, g, w_in, q_norm, k_norm, seq):
    n, d = x2d.shape
    q_w, kv_w, b_w = A_HEADS * HEAD_DIM, A_KV_HEADS * HEAD_DIM, d // 2
    qk_w = q_w + kv_w
    tm = TM_PROJ
    gq = jnp.tile(q_norm.astype(F32), A_HEADS) * (HEAD_DIM ** -0.5 * LOG2E)
    gk = jnp.tile(k_norm.astype(F32), A_KV_HEADS)
    gqk = jnp.concatenate([gq, gk])[None, :]
    hid = jnp.arange(qk_w) // HEAD_DIM
    bd = (hid[:, None] == hid[None, :]).astype(BF16) * (1.0 / HEAD_DIM)
    pos = jnp.arange(seq, dtype=jnp.int32)
    row, col = pos // GRID_W, pos % GRID_W
    m = ROPE_HALF
    inv = ROPE_THETA ** (-jnp.arange(m, dtype=F32) / m)
    ang_r = row.astype(F32)[:, None] * inv[None, :]
    ang_c = col.astype(F32)[:, None] * inv[None, :]
    zero = jnp.zeros_like(ang_r)
    cos64 = jnp.concatenate([jnp.cos(ang_r), jnp.cos(ang_r), jnp.cos(ang_c), jnp.cos(ang_c)], axis=1)
    s1_64 = jnp.concatenate([-jnp.sin(ang_r), zero, -jnp.sin(ang_c), zero], axis=1)
    s2_64 = jnp.concatenate([zero, jnp.sin(ang_r), zero, jnp.sin(ang_c)], axis=1)
    rep = LANES // HEAD_DIM
    cos_t, s1_t, s2_t = (jnp.tile(t, (1, rep)) for t in (cos64, s1_64, s2_64))
    nseq = seq // tm
    tab_spec = pl.BlockSpec((tm, LANES), lambda i: (i % nseq, 0))
    full = lambda shape: pl.BlockSpec(shape, lambda i: (0,) * len(shape))
    kern = functools.partial(_even_in_kernel, qk_w=qk_w, q_w=q_w, kv_w=kv_w, b_w=b_w)
    return pl.pallas_call(
        kern,
        grid=(n // tm,),
        in_specs=[pl.BlockSpec((tm, d), lambda i: (i, 0)), full((1, d)), full(w_in.shape),
                  full((1, qk_w)), full((qk_w, qk_w)), tab_spec, tab_spec, tab_spec],
        out_specs=[_chunkT_spec(q_w, tm, nseq), pl.BlockSpec((tm, kv_w), lambda i: (i, 0)),
                   _chunkT_spec(kv_w, tm, nseq), pl.BlockSpec((tm, b_w), lambda i: (i, 0))],
        out_shape=[jax.ShapeDtypeStruct((n // seq, nseq, q_w, tm), BF16), jax.ShapeDtypeStruct((n, kv_w), BF16),
                   jax.ShapeDtypeStruct((n // seq, nseq, kv_w, tm), BF16), jax.ShapeDtypeStruct((n, b_w), F32)],
        compiler_params=_cparams("parallel"),
        name="even_in_proj",
    )(x2d, g[None, :].astype(F32), w_in.astype(BF16), gqk, bd, cos_t, s1_t, s2_t)


def _flash_pipelined(nk, n_lead, n_maps, score_fn, shift_fn, value_fn, s_a, s_b, tq, dv):
    assert nk % 2 == 0 and n_lead % 2 == 0 and nk - n_lead >= 2
    ones = jnp.ones((BF16_ROWS, s_a.shape[1]), BF16)

    def scores(t, s_ref, lead):
        shift = shift_fn(t, lead)
        cms = []
        for c in range(n_maps):
            s = score_fn(t, c, lead)
            s_ref[c] = s
            cm = jnp.max(s, axis=0, keepdims=True)
            cms.append(cm if shift is None else cm + shift)
        return tuple(cms)

    def softmax(t, s_ref, cms, state, lead):
        shift = shift_fn(t, lead)
        vals = jnp.concatenate([value_fn(t), ones], axis=0)
        new = []
        for c in range(n_maps):
            m, acc = state[c]
            m_new = jnp.maximum(m, cms[c])
            alpha = jnp.exp2(m - m_new)
            p = jnp.exp2(s_ref[c] - (m_new if shift is None else m_new - shift))
            acc = alpha * acc + jnp.dot(vals, p.astype(BF16), preferred_element_type=F32)
            new.append((m_new, acc))
        return tuple(new)

    state = tuple((jnp.full((1, tq), NEG_BIG, F32), jnp.zeros((dv + BF16_ROWS, tq), F32)) for _ in range(n_maps))
    bufs = (s_a, s_b)
    cms = scores(0, s_a, n_lead > 0)
    for t in range(n_lead):
        nxt = scores(t + 1, bufs[(t + 1) % 2], t + 1 < n_lead)
        state = softmax(t, bufs[t % 2], cms, state, True)
        cms = nxt

    def pair(u, carry):
        cms_a, state = carry
        t = n_lead + 2 * u
        cms_b = scores(t + 1, s_b, False)
        state = softmax(t, s_a, cms_a, state, False)
        cms_a = scores(t + 2, s_a, False)
        state = softmax(t + 1, s_b, cms_b, state, False)
        return cms_a, state

    cms_a, state = lax.fori_loop(0, (nk - n_lead) // 2 - 1, pair, (cms, state))
    cms_b = scores(nk - 1, s_b, False)
    state = softmax(nk - 2, s_a, cms_a, state, False)
    state = softmax(nk - 1, s_b, cms_b, state, False)
    return tuple(acc[:dv] / acc[dv:dv + 1] for _, acc in state)


def _gqa_kernel(qT_ref, k_ref, vT_ref, o_ref, s_a, s_b, *, nk, tk, nh):
    hd = HEAD_DIM
    tq = qT_ref.shape[3]
    kv = pl.program_id(1) // (A_GROUP // nh)
    row_head = lax.broadcasted_iota(jnp.int32, (A_KV_HEADS * hd, tq), 0) // hd
    qs = []
    for h in range(nh):
        qh = qT_ref[0, 0, h * hd:(h + 1) * hd, :]
        qs.append(jnp.where(row_head == kv, jnp.concatenate([qh] * A_KV_HEADS, axis=0), jnp.zeros((), BF16)))

    def score_fn(t, c, lead):
        kc = k_ref[0, pl.ds(pl.multiple_of(t * tk, tk), tk), :]
        return jnp.dot(kc, qs[c], preferred_element_type=F32)

    res = _flash_pipelined(nk, 0, nh, score_fn, lambda t, lead: None, lambda t: vT_ref[0, t], s_a, s_b, tq, hd)
    for h, o in enumerate(res):
        o_ref[0, 0, h * hd:(h + 1) * hd, :] = o.astype(o_ref.dtype)


def _gqa_attention(qT, k, vT, batch, seq):
    hd, nh = HEAD_DIM, HEADS_PER_STEP_A
    tq = qT.shape[3]
    tk = vT.shape[3]
    nk = seq // tk
    hsteps = A_HEADS // nh
    per_kv = A_GROUP // nh
    k3 = k.reshape(batch, seq, A_KV_HEADS * hd)
    return pl.pallas_call(
        functools.partial(_gqa_kernel, nk=nk, tk=tk, nh=nh),
        grid=(batch, hsteps, seq // tq),
        in_specs=[pl.BlockSpec((1, 1, nh * hd, tq), lambda b, h, i: (b, i, h, 0)),
                  pl.BlockSpec((1, seq, A_KV_HEADS * hd), lambda b, h, i: (b, 0, 0)),
                  pl.BlockSpec((1, nk, hd, tk), lambda b, h, i: (b, 0, h // per_kv, 0))],
        out_specs=pl.BlockSpec((1, 1, nh * hd, tq), lambda b, h, i: (b, i, h, 0)),
        out_shape=jax.ShapeDtypeStruct(qT.shape, BF16),
        scratch_shapes=[pltpu.VMEM((nh, tk, tq), F32), pltpu.VMEM((nh, tk, tq), F32)],
        compiler_params=_cparams("parallel", "parallel", "parallel"),
        name="gqa_attention",
    )(qT, k3, vT)


def _conv_kernel(prev_ref, cur_ref, next_ref, w_ref, b_ref, g_ref, beta_ref, bd_ref, o_ref, xe_ref, xs_ref, *, ts):
    i = pl.program_id(1)
    last = pl.num_programs(1) - 1
    halo = CONV_HALO
    xe_ref[0:halo, :] = jnp.where(i > 0, prev_ref[0], 0.0)
    xe_ref[halo:halo + ts, :] = cur_ref[0]
    xe_ref[halo + ts:halo + ts + halo, :] = jnp.where(i < last, next_ref[0], 0.0)
    rows = xs_ref.shape[1]
    for rho in range(1, SUBLANES):
        xs_ref[rho - 1] = xe_ref[rho:rho + rows, :]
    w = w_ref[...]
    first = halo - CONV_WIDTH // 2
    for r0 in range(0, ts, CONV_SUB):
        acc = jnp.zeros((CONV_SUB, w.shape[1]), F32) + b_ref[...]
        for j in range(CONV_WIDTH):
            rho = (first + j) % SUBLANES
            base = r0 + first + j - rho
            src = xe_ref[base:base + CONV_SUB, :] if rho == 0 else xs_ref[rho - 1, base:base + CONV_SUB, :]
            acc = acc + w[j:j + 1, :] * src
        mu = _split_dot(acc, bd_ref[...])
        cen = acc - mu
        var = _split_dot(cen * cen, bd_ref[...])
        hn = cen * lax.rsqrt(var + 1e-5) * g_ref[...] + beta_ref[...]
        o_ref[0, r0:r0 + CONV_SUB, :] = (hn * _sigmoid(hn)).astype(o_ref.dtype)


def _conformer_conv(glu, conv_w, conv_b, gn_g, gn_b, batch, seq):
    c = glu.shape[1]
    ts, halo = TS_CONV, CONV_HALO
    x3 = glu.reshape(batch, seq, c)
    hpb = ts // halo
    nh = seq // halo
    gid = jnp.arange(c) // B_GROUP_WIDTH
    bd = (gid[:, None] == gid[None, :]).astype(BF16) * (1.0 / B_GROUP_WIDTH)
    wpad = jnp.zeros((32, c), F32).at[:CONV_WIDTH].set(conv_w.astype(F32))
    vec = lambda a: a[None, :].astype(F32)
    full = lambda shape: pl.BlockSpec(shape, lambda b, i: (0,) * len(shape))
    out = pl.pallas_call(
        functools.partial(_conv_kernel, ts=ts),
        grid=(batch, seq // ts),
        in_specs=[pl.BlockSpec((1, halo, c), lambda b, i: (b, jnp.maximum(i * hpb - 1, 0), 0)),
                  pl.BlockSpec((1, ts, c), lambda b, i: (b, i, 0)),
                  pl.BlockSpec((1, halo, c), lambda b, i: (b, jnp.minimum((i + 1) * hpb, nh - 1), 0)),
                  full((32, c)), full((1, c)), full((1, c)), full((1, c)), full((c, c))],
        out_specs=pl.BlockSpec((1, ts, c), lambda b, i: (b, i, 0)),
        out_shape=jax.ShapeDtypeStruct((batch, seq, c), BF16),
        scratch_shapes=[pltpu.VMEM((ts + 2 * halo, c), F32),
                        pltpu.VMEM((SUBLANES - 1, ts + 2 * halo - SUBLANES, c), F32)],
        compiler_params=_cparams("parallel", "parallel"),
        name="conformer_conv",
    )(x3, x3, x3, wpad, vec(conv_b), vec(gn_g), vec(gn_b), bd)
    return out.reshape(batch * seq, c)


def _route(logits):
    lane = lax.broadcasted_iota(jnp.int32, logits.shape, 1)
    big = jnp.int32(1 << 20)
    ninf = jnp.float32(-jnp.inf)
    gmask = (lane >= MOE_EXPERTS) & (lane < MOE_EXPERTS + MOE_GROUPS)
    lg = jnp.where(gmask, logits, ninf)
    mg = jnp.max(lg, axis=-1, keepdims=True)
    sg = jnp.sum(jnp.exp(lg - mg), axis=-1, keepdims=True)
    g_val = 1.0 / sg
    g_idx = jnp.min(jnp.where(lg == mg, lane - MOE_EXPERTS, big), axis=-1, keepdims=True)
    emask = (lane < MOE_EXPERTS) & ((lane // MOE_PER_GROUP) == g_idx)
    le = jnp.where(emask, logits, ninf)
    m1 = jnp.max(le, axis=-1, keepdims=True)
    se = jnp.sum(jnp.exp(le - m1), axis=-1, keepdims=True)
    i1 = jnp.min(jnp.where(le == m1, lane, big), axis=-1, keepdims=True)
    le2 = jnp.where(lane == i1, ninf, le)
    m2 = jnp.max(le2, axis=-1, keepdims=True)
    i2 = jnp.min(jnp.where(le2 == m2, lane, big), axis=-1, keepdims=True)
    e1 = 1.0 / se
    e2 = jnp.exp(m2 - m1) / se
    norm = g_val / (e1 + e2)
    ids = jnp.where(lane == 0, i1, i2).astype(F32)
    gates = jnp.where(lane == MOE_TOPK, e1 * norm, e2 * norm)
    return jnp.where(lane < MOE_TOPK, ids, gates)


def _out_router_kernel(*refs, has_yb):
    if has_yb:
        yT_ref, yb_ref, x_ref, wa_ref, wb_ref = refs[:5]
        rest = refs[5:]
    else:
        yT_ref, x_ref, wa_ref = refs[:3]
        rest = refs[3:]
    g_ref, wrh_ref, wrl_ref, br_ref, x1_ref, h_ref, route_ref = rest
    ya = yT_ref[0, 0].astype(F32).T.astype(BF16)
    x1 = x_ref[...] + jnp.dot(ya, wa_ref[...], preferred_element_type=F32)
    if has_yb:
        x1 = x1 + jnp.dot(yb_ref[...], wb_ref[...], preferred_element_type=F32)
    x1_ref[...] = x1
    h = _rms_rows(x1, g_ref[...], EPS)
    _rows_to_tiles(h_ref, h)
    hi = h.astype(BF16)
    lo = (h - hi.astype(F32)).astype(BF16)
    logits = (jnp.dot(hi, wrh_ref[...], preferred_element_type=F32)
              + jnp.dot(hi, wrl_ref[...], preferred_element_type=F32)
              + jnp.dot(lo, wrh_ref[...], preferred_element_type=F32) + br_ref[...])
    route_ref[...] = _route(logits).T[:SUBLANES]


def _out_router(yT, yb, x2d, w_out, g_ffn, w_group, b_group, w_expert, b_expert):
    n, d = x2d.shape
    tm = yT.shape[3]
    nseq = yT.shape[1]
    wa = yT.shape[2]
    wb16 = w_out.astype(BF16)
    wr = jnp.zeros((d, LANES), F32).at[:, :MOE_EXPERTS].set(w_expert.astype(F32))
    wr = wr.at[:, MOE_EXPERTS:MOE_EXPERTS + MOE_GROUPS].set(w_group.astype(F32))
    wrh = wr.astype(BF16)
    wrl = (wr - wrh.astype(F32)).astype(BF16)
    br = jnp.zeros((1, LANES), F32).at[0, :MOE_EXPERTS].set(b_expert.astype(F32))
    br = br.at[0, MOE_EXPERTS:MOE_EXPERTS + MOE_GROUPS].set(b_group.astype(F32))
    full = lambda shape: pl.BlockSpec(shape, lambda i: (0,) * len(shape))
    row = lambda w: pl.BlockSpec((tm, w), lambda i: (i, 0))
    has_yb = yb is not None
    if has_yb:
        args = [yT, yb, x2d, wb16[:wa], wb16[wa:]]
        specs = [_chunkT_spec(wa, tm, nseq), row(yb.shape[1]), row(d), full((wa, d)), full((w_out.shape[0] - wa, d))]
    else:
        assert wa == w_out.shape[0]
        args = [yT, x2d, wb16]
        specs = [_chunkT_spec(wa, tm, nseq), row(d), full((wa, d))]
    return pl.pallas_call(
        functools.partial(_out_router_kernel, has_yb=has_yb),
        grid=(n // tm,),
        in_specs=specs + [full((1, d)), full((d, LANES)), full((d, LANES)), full((1, LANES))],
        out_specs=[row(d), pl.BlockSpec((tm * SUBLANES, LANES), lambda i: (i, 0)),
                   pl.BlockSpec((SUBLANES, tm), lambda i: (0, i))],
        out_shape=[jax.ShapeDtypeStruct((n, d), F32), jax.ShapeDtypeStruct((n * SUBLANES, LANES), F32),
                   jax.ShapeDtypeStruct((SUBLANES, n), F32)],
        compiler_params=_cparams("parallel"),
        name="out_proj_router",
    )(*args, g_ffn[None, :].astype(F32), wrh, wrl, br)


def _moe_dispatch(route, n):
    ids = route[:MOE_TOPK].astype(jnp.int32)
    gates = route[MOE_TOPK:2 * MOE_TOPK]
    bm = MOE_BM
    a = n * MOE_TOPK
    p = a + MOE_EXPERTS * bm
    flat_e = ids.reshape(-1)
    flat_w = gates.reshape(-1)
    order = jnp.argsort(flat_e).astype(jnp.int32)
    experts = jnp.arange(MOE_EXPERTS, dtype=jnp.int32)
    counts = jnp.sum((flat_e[:, None] == experts[None, :]).astype(jnp.int32), axis=0)
    pcounts = (counts + bm - 1) // bm * bm
    starts = jnp.cumsum(counts) - counts
    pends = jnp.cumsum(pcounts)
    pstarts = pends - pcounts
    n_blk = p // bm
    blk_start = jnp.arange(n_blk, dtype=jnp.int32) * bm
    blk_e = jnp.minimum(jnp.sum((blk_start[:, None] >= pends[None, :]).astype(jnp.int32), axis=1),
                        MOE_EXPERTS - 1)
    off = (blk_start - pstarts[blk_e])[:, None] + jnp.arange(bm, dtype=jnp.int32)[None, :]
    valid = (off < counts[blk_e][:, None]).reshape(p)
    sidx = jnp.clip(starts[blk_e][:, None] + off, 0, a - 1).reshape(p)
    assign = order[sidx]
    tok = assign % n
    slot = assign // n
    src_tok = jnp.where(valid, tok, 0)
    dump = a + jnp.cumsum(1 - valid.astype(jnp.int32)) - 1
    dst_row = jnp.where(valid, slot * n + tok, dump).astype(jnp.int32)
    row_w = jnp.where(valid, flat_w[assign], 0.0)
    tok_bits = (n - 1).bit_length()
    src_tok = jnp.concatenate([jnp.zeros((bm,), jnp.int32), src_tok])
    dst_row = jnp.concatenate([p + jnp.arange(bm, dtype=jnp.int32), dst_row])
    assert tok_bits + (p + 2 * bm - 1).bit_length() <= 32
    packed = lax.bitcast_convert_type(
        src_tok.astype(jnp.uint32) | (dst_row.astype(jnp.uint32) << tok_bits), jnp.int32)
    return blk_e, packed, row_w[:, None], tok_bits


def _moe_kernel(be_ref, idx_ref, h_hbm, w1_ref, w3_ref, w2_ref, rw_ref, y_hbm,
                xbuf, obuf, gsem, ssem, *, tok_bits, spare_row):
    del be_ref
    bm = MOE_BM
    i = pl.program_id(0)
    nb = pl.num_programs(0)
    slot = i % 2
    tok_mask = (1 << tok_bits) - 1

    tile = lambda r: pl.ds(pl.multiple_of(r * SUBLANES, SUBLANES), SUBLANES)

    def gather_row(blk, sl, r):
        t = idx_ref[(blk + 1) * bm + r] & tok_mask
        pltpu.make_async_copy(h_hbm.at[tile(t)], xbuf.at[sl, tile(r)], gsem.at[sl]).start()

    def scatter_row(blk, sl, r):
        t = lax.shift_right_logical(idx_ref[(blk + 1) * bm + r], tok_bits)
        pltpu.make_async_copy(obuf.at[sl, tile(r)], y_hbm.at[tile(t)], ssem.at[sl]).start()

    def row_loop(body):
        lax.fori_loop(0, bm, lambda r, c: (body(r), c)[1], 0, unroll=MOE_DMA_UNROLL)

    def gather_wait(sl):
        pltpu.make_async_copy(h_hbm.at[pl.ds(0, bm * SUBLANES)], xbuf.at[sl], gsem.at[sl]).wait()

    def scatter_wait(sl):
        pltpu.make_async_copy(obuf.at[sl], y_hbm.at[pl.ds(0, bm * SUBLANES)], ssem.at[sl]).wait()

    @pl.when(i == 0)
    def _():
        obuf[...] = jnp.zeros(obuf.shape, obuf.dtype)
        row_loop(lambda r: gather_row(0, 0, r))
        row_loop(lambda r: pltpu.make_async_copy(obuf.at[0, tile(r)], y_hbm.at[tile(spare_row + r)],
                                                 ssem.at[0]).start())

    gather_wait(slot)
    nxt = jnp.minimum(i + 1, nb - 1)
    for r in range(bm):
        gather_row(nxt, 1 - slot, r)
        scatter_row(i - 1, 1 - slot, r)
    x = _rows_from_tiles(xbuf.at[slot]).astype(BF16)
    h1 = jnp.dot(x, w1_ref[0, 0].astype(BF16), preferred_element_type=F32)
    h3 = jnp.dot(x, w3_ref[0, 0].astype(BF16), preferred_element_type=F32)
    hm = (h1 * _sigmoid(h1) * h3).astype(BF16)
    y = jnp.dot(hm, w2_ref[0, 0].astype(BF16), preferred_element_type=F32) * rw_ref[...]
    scatter_wait(slot)
    _rows_to_tiles(obuf.at[slot], y)

    @pl.when(i == nb - 1)
    def _():
        row_loop(lambda r: scatter_row(i, slot, r))
        scatter_wait(slot)
        scatter_wait(1 - slot)
        gather_wait(1 - slot)


def _moe_ffn(h_tiles, route, w1, w3, w2, layer):
    d = SUBLANES * LANES
    n = h_tiles.shape[0] // SUBLANES
    assert w1.shape[2] == d and h_tiles.shape[1] == LANES
    bm = MOE_BM
    hid = w1.shape[-1]
    blk_e, packed, row_w, tok_bits = _moe_dispatch(route, n)
    n_blk = blk_e.shape[0]
    rows_real = n * MOE_TOPK + MOE_EXPERTS * bm
    rows_out = rows_real + 2 * bm
    grid_spec = pltpu.PrefetchScalarGridSpec(
        num_scalar_prefetch=2,
        grid=(n_blk,),
        in_specs=[pl.BlockSpec(memory_space=pl.ANY),
                  pl.BlockSpec((1, 1, d, hid), lambda i, be, ix: (layer, be[i], 0, 0)),
                  pl.BlockSpec((1, 1, d, hid), lambda i, be, ix: (layer, be[i], 0, 0)),
                  pl.BlockSpec((1, 1, hid, d), lambda i, be, ix: (layer, be[i], 0, 0)),
                  pl.BlockSpec((bm, 1), lambda i, be, ix: (i, 0))],
        out_specs=pl.BlockSpec(memory_space=pl.ANY),
        scratch_shapes=[pltpu.VMEM((2, bm * SUBLANES, LANES), F32), pltpu.VMEM((2, bm * SUBLANES, LANES), F32),
                        pltpu.SemaphoreType.DMA((2,)), pltpu.SemaphoreType.DMA((2,))],
    )
    return pl.pallas_call(
        functools.partial(_moe_kernel, tok_bits=tok_bits, spare_row=rows_real + bm),
        grid_spec=grid_spec,
        out_shape=jax.ShapeDtypeStruct((rows_out * SUBLANES, LANES), F32),
        compiler_params=_cparams("arbitrary"),
        name="moe_ffn",
    )(blk_e, packed, h_tiles, w1, w3, w2, row_w)


def _odd_in_kernel(x_ref, y0_ref, y1_ref, g_ref, w_ref, xo_ref, q_ref, k_ref, v_ref, *, cw):
    x = x_ref[...] + _rows_from_tiles(y0_ref) + _rows_from_tiles(y1_ref)
    xo_ref[...] = x
    h = _rms_rows(x, g_ref[...], EPS)
    z = jnp.dot(h.astype(BF16), w_ref[...], preferred_element_type=F32)
    q_ref[0, 0] = (z[:, :cw] * (C_HEAD_DIM ** -0.5 * LOG2E)).T.astype(BF16)
    k_ref[...] = z[:, cw:2 * cw].astype(BF16)
    v_ref[0, 0] = z[:, 2 * cw:3 * cw].T.astype(BF16)


def _odd_in(x2d, ybuf, g, w_in, seq):
    n, d = x2d.shape
    tm = TM_PROJ
    cw = w_in.shape[1] // 3
    nt = n // tm
    row = lambda w: pl.BlockSpec((tm, w), lambda i: (i, 0))
    full = lambda shape: pl.BlockSpec(shape, lambda i: (0,) * len(shape))
    return pl.pallas_call(
        functools.partial(_odd_in_kernel, cw=cw),
        grid=(nt,),
        in_specs=[row(d), pl.BlockSpec((tm * SUBLANES, LANES), lambda i: (i, 0)),
                  pl.BlockSpec((tm * SUBLANES, LANES), lambda i: (i + nt, 0)), full((1, d)), full(w_in.shape)],
        out_specs=[row(d), _chunkT_spec(cw, tm, seq // tm), row(cw), _chunkT_spec(cw, tm, seq // tm)],
        out_shape=[jax.ShapeDtypeStruct((n, d), F32), jax.ShapeDtypeStruct((n // seq, seq // tm, cw, tm), BF16),
                   jax.ShapeDtypeStruct((n, cw), BF16), jax.ShapeDtypeStruct((n // seq, seq // tm, cw, tm), BF16)],
        compiler_params=_cparams("parallel"),
        name="odd_in_proj",
    )(x2d, ybuf, ybuf, g[None, :].astype(F32), w_in.astype(BF16))


def _t5_bucket(rel):
    nb = REL_BUCKETS // 2
    max_exact = nb // 2
    ret = jnp.where(rel > 0, nb, 0).astype(jnp.int32)
    n = jnp.abs(rel)
    nf = jnp.maximum(n, 1).astype(F32)
    large = max_exact + (jnp.log(nf / max_exact) / math.log(REL_MAX_DIST / max_exact) * (nb - max_exact)).astype(jnp.int32)
    large = jnp.minimum(large, nb - 1)
    return ret + jnp.where(n < max_exact, n, large)


def _diff_kernel(cfar_ref, lam_ref, qT_ref, k_ref, vT_ref, band_ref, gsub_ref, o_ref, s_a, s_b, *, nk, tk):
    head = pl.program_id(1)
    i = pl.program_id(2)
    hd = C_HEAD_DIM
    tq = qT_ref.shape[3]
    dv = vT_ref.shape[2]
    qT = qT_ref[0, 0]
    row_map = lax.broadcasted_iota(jnp.int32, qT.shape, 0) // hd
    qs = [jnp.where(row_map == c, qT, jnp.zeros((), BF16)) for c in range(2)]

    chunk = lambda t: (i + (nk - 1) + t) % nk

    def score_fn(t, c, lead):
        j = chunk(t)
        s = jnp.dot(k_ref[0, pl.ds(pl.multiple_of(j * tk, tk), tk), :], qs[c], preferred_element_type=F32)
        if lead:
            s = s + band_ref[0, jnp.clip(j - i, -2, 2) + 2]
        return s

    def shift_fn(t, lead):
        if lead:
            return None
        return jnp.where(chunk(t) < i, cfar_ref[head, 0], cfar_ref[head, 1])

    o0, o1 = _flash_pipelined(nk, DIFF_LEAD, 2, score_fn, shift_fn, lambda t: vT_ref[0, chunk(t)],
                              s_a, s_b, tq, dv)
    o = o0 - lam_ref[...] * o1
    ms = jnp.mean(o * o, axis=0, keepdims=True)
    o_ref[0, 0] = (o * lax.rsqrt(ms + 1e-5) * gsub_ref[...]).astype(o_ref.dtype)


def _bias_band(rel_bias, t):
    assert t >= REL_MAX_DIST
    rb = rel_bias.astype(F32) * LOG2E
    period = 2 * t
    m = jnp.arange(period, dtype=jnp.int32)
    d3 = jnp.arange(-1, 2, dtype=jnp.int32)[:, None] * t
    rel = jnp.where(m[None, :] < t, d3 - m[None, :], d3 + (period - m[None, :]))
    vec = rb[_t5_bucket(rel)].transpose(2, 0, 1)
    nh = vec.shape[0]
    skew = jnp.tile(vec, (1, 1, t))[:, :, :t * (period - 1)].reshape(nh, 3, t, period - 1)[..., :t]
    far = lambda r: jnp.broadcast_to(rb[_t5_bucket(jnp.int32(r))][:, None, None, None], (nh, 1, t, t))
    return jnp.concatenate([far(-2 * t), skew, far(2 * t)], axis=1)


def _diff_attention(qT, k, vT, lam, lam_init, subln, rel_bias, batch, seq):
    hd = C_HEAD_DIM
    dv = 2 * hd
    tq = qT.shape[3]
    tk = vT.shape[3]
    nk = seq // tk
    assert tk == tq
    k3 = k.reshape(batch, seq, C_HEADS * dv)
    assert nk >= DIFF_LEAD + 2
    band = _bias_band(rel_bias, tk)
    rb = rel_bias.astype(F32) * LOG2E
    cfar = jnp.stack([rb[_t5_bucket(jnp.int32(-2 * tk))], rb[_t5_bucket(jnp.int32(2 * tk))]], axis=1)
    lam_vec = jnp.full((1, tq), lam, F32)
    gsub = (subln.astype(F32) * (1.0 - lam_init))[:, None]
    return pl.pallas_call(
        functools.partial(_diff_kernel, nk=nk, tk=tk),
        grid=(batch, C_HEADS, seq // tq),
        in_specs=[pl.BlockSpec(memory_space=pltpu.SMEM),
                  pl.BlockSpec((1, tq), lambda b, h, i: (0, 0)),
                  pl.BlockSpec((1, 1, dv, tq), lambda b, h, i: (b, i, h, 0)),
                  pl.BlockSpec((1, seq, dv), lambda b, h, i: (b, 0, h)),
                  pl.BlockSpec((1, nk, dv, tk), lambda b, h, i: (b, 0, h, 0)),
                  pl.BlockSpec((1, 5, tk, tq), lambda b, h, i: (h, 0, 0, 0)),
                  pl.BlockSpec((dv, 1), lambda b, h, i: (0, 0))],
        out_specs=pl.BlockSpec((1, 1, dv, tq), lambda b, h, i: (b, i, h, 0)),
        out_shape=jax.ShapeDtypeStruct(qT.shape, BF16),
        scratch_shapes=[pltpu.VMEM((2, tk, tq), F32), pltpu.VMEM((2, tk, tq), F32)],
        compiler_params=_cparams("parallel", "parallel", "parallel"),
        name="diff_attention",
    )(cfar, lam_vec, qT, k3, vT, band, gsub)


def _final_kernel(x_ref, y0_ref, y1_ref, g_ref, o_ref):
    x = x_ref[...] + _rows_from_tiles(y0_ref) + _rows_from_tiles(y1_ref)
    o_ref[...] = _rms_rows(x, g_ref[...], EPS)


def _final(x2d, ybuf, g):
    n, d = x2d.shape
    tm = TM_PROJ
    nt = n // tm
    row = pl.BlockSpec((tm, d), lambda i: (i, 0))
    return pl.pallas_call(
        _final_kernel,
        grid=(nt,),
        in_specs=[row, pl.BlockSpec((tm * SUBLANES, LANES), lambda i: (i, 0)),
                  pl.BlockSpec((tm * SUBLANES, LANES), lambda i: (i + nt, 0)), pl.BlockSpec((1, d), lambda i: (0, 0))],
        out_specs=row,
        out_shape=jax.ShapeDtypeStruct((n, d), F32),
        compiler_params=_cparams("parallel"),
        name="final_norm",
    )(x2d, ybuf, ybuf, g[None, :].astype(F32))


def kernel(x, norm_mix, norm_ffn, norm_final, ev_w_in, ev_q_norm, ev_k_norm, ev_conv_w, ev_conv_b, ev_gn_g, ev_gn_b, ev_w_out, od_w_in, od_lam_q1, od_lam_k1, od_lam_q2, od_lam_k2, od_subln, od_w_out, rel_bias, moe_w_group, moe_b_group, moe_w_expert, moe_b_expert, moe_w1, moe_w3, moe_w2):
    batch, seq, d = x.shape
    n = batch * seq
    x2d = x.reshape(n, d).astype(F32)

    qT, k, vT, glu = _even_in(x2d, norm_mix[0], ev_w_in[0], ev_q_norm[0], ev_k_norm[0], seq)
    yaT = _gqa_attention(qT, k, vT, batch, seq)
    yb = _conformer_conv(glu, ev_conv_w[0], ev_conv_b[0], ev_gn_g[0], ev_gn_b[0], batch, seq)
    x1, h, route = _out_router(yaT, yb, x2d, ev_w_out[0], norm_ffn[0],
                                    moe_w_group[0], moe_b_group[0], moe_w_expert[0], moe_b_expert[0])
    ybuf = _moe_ffn(h, route, moe_w1, moe_w3, moe_w2, 0)

    x2, qT, k, vT = _odd_in(x1, ybuf, norm_mix[1], od_w_in[0], seq)
    layer_idx = 1
    lam_init = 0.8 - 0.6 * math.exp(-0.3 * layer_idx)
    lam = (jnp.exp(jnp.sum(od_lam_q1[0].astype(F32) * od_lam_k1[0].astype(F32)))
           - jnp.exp(jnp.sum(od_lam_q2[0].astype(F32) * od_lam_k2[0].astype(F32))) + lam_init)
    oT = _diff_attention(qT, k, vT, lam, lam_init, od_subln[0], rel_bias, batch, seq)
    x3, h, route = _out_router(oT, None, x2, od_w_out[0], norm_ffn[1],
                                    moe_w_group[1], moe_b_group[1], moe_w_expert[1], moe_b_expert[1])
    ybuf = _moe_ffn(h, route, moe_w1, moe_w3, moe_w2, 1)
    out = _final(x3, ybuf, norm_final)
    return out.reshape(batch, seq, d)
```

```python
import functools
import math

import jax
import jax.numpy as jnp
from jax import lax
from jax.experimental import pallas as pl
from jax.experimental.pallas import tpu as pltpu

F32 = jnp.float32
BF16 = jnp.bfloat16

GRID_W = 64
HEAD_DIM = 64
EPS = 1e-6
A_HEADS = 8
A_KV_HEADS = 2
A_GROUP = A_HEADS // A_KV_HEADS
ROPE_THETA = 10000.0
ROPE_HALF = HEAD_DIM // 4
B_GROUP_WIDTH = 64
CONV_WIDTH = 31
CONV_HALO = 16
C_HEADS = 8
C_HEAD_DIM = 64
REL_BUCKETS = 32
REL_MAX_DIST = 128
MOE_GROUPS = 4
MOE_PER_GROUP = 8
MOE_EXPERTS = MOE_GROUPS * MOE_PER_GROUP
MOE_TOPK = 2
LOG2E = 1.4426950408889634
NEG_BIG = -1e30
LANES = 128
SUBLANES = 8
MOE_DMA_UNROLL = 8

TM_PROJ = 512
HEADS_PER_STEP_A = 2
DIFF_LEAD = 4
BF16_ROWS = 16
TS_CONV = 256
CONV_SUB = 64
MOE_BM = 128
VMEM_LIMIT = 56 * 1024 * 1024


def _cparams(*sem):
    return pltpu.CompilerParams(dimension_semantics=sem, vmem_limit_bytes=VMEM_LIMIT)


def _split_dot(a, b_bf16):
    hi = a.astype(BF16)
    lo = (a - hi.astype(F32)).astype(BF16)
    return (jnp.dot(hi, b_bf16, preferred_element_type=F32)
            + jnp.dot(lo, b_bf16, preferred_element_type=F32))


def _rms_rows(x, g, eps):
    ms = jnp.mean(x * x, axis=-1, keepdims=True)
    return x * lax.rsqrt(ms + eps) * g


def _sigmoid(x):
    return 1.0 / (1.0 + jnp.exp(-x))


def _chunkT_spec(width, tm, nseq):
    return pl.BlockSpec((1, 1, width, tm), lambda i: (i // nseq, i % nseq, 0, 0))


def _rows_from_tiles(ref):
    rows = ref.shape[0] // SUBLANES
    return jnp.concatenate([ref[pl.ds(s, rows, stride=SUBLANES), :] for s in range(SUBLANES)], axis=1)


def _rows_to_tiles(ref, val):
    rows = val.shape[0]
    for s in range(SUBLANES):
        ref[pl.ds(s, rows, stride=SUBLANES), :] = val[:, s * LANES:(s + 1) * LANES]


def _even_in_kernel(x_ref, g_ref, w_ref, gqk_ref, bd_ref, cos_ref, s1_ref, s2_ref,
                    q_ref, k_ref, v_ref, glu_ref, *, qk_w, q_w, kv_w, b_w):
    h = _rms_rows(x_ref[...], g_ref[...], EPS)
    z = jnp.dot(h.astype(BF16), w_ref[...], preferred_element_type=F32)
    qk = z[:, :qk_w]
    ms = _split_dot(qk * qk, bd_ref[...])
    qkn = qk * lax.rsqrt(ms + EPS) * gqk_ref[...]
    cos, s1, s2 = cos_ref[...], s1_ref[...], s2_ref[...]
    parts = []
    for c in range(qk_w // LANES):
        blk = qkn[:, c * LANES:(c + 1) * LANES]
        up = pltpu.roll(blk, LANES - ROPE_HALF, 1)
        dn = pltpu.roll(blk, ROPE_HALF, 1)
        parts.append(blk * cos + up * s1 + dn * s2)
    rot = jnp.concatenate(parts, axis=1)
    q_ref[0, 0] = rot[:, :q_w].T.astype(BF16)
    k_ref[...] = rot[:, q_w:qk_w].astype(BF16)
    v_ref[0, 0] = z[:, qk_w:qk_w + kv_w].T.astype(BF16)
    a = z[:, qk_w + kv_w:qk_w + kv_w + b_w]
    gate = z[:, qk_w + kv_w + b_w:qk_w + kv_w + 2 * b_w]
    glu_ref[...] = a * _sigmoid(gate)


def _even_in(x2d, g, w_in, q_norm, k_norm, seq):
    n, d = x2d.shape
    q_w, kv_w, b_w = A_HEADS * HEAD_DIM, A_KV_HEADS * HEAD_DIM, d // 2
    qk_w = q_w + kv_w
    tm = TM_PROJ
    gq = jnp.tile(q_norm.astype(F32), A_HEADS) * (HEAD_DIM ** -0.5 * LOG2E)
    gk = jnp.tile(k_norm.astype(F32), A_KV_HEADS)
    gqk = jnp.concatenate([gq, gk])[None, :]
    hid = jnp.arange(qk_w) // HEAD_DIM
    bd = (hid[:, None] == hid[None, :]).astype(BF16) * (1.0 / HEAD_DIM)
    pos = jnp.arange(seq, dtype=jnp.int32)
    row, col = pos // GRID_W, pos % GRID_W
    m = ROPE_HALF
    inv = ROPE_THETA ** (-jnp.arange(m, dtype=F32) / m)
    ang_r = row.astype(F32)[:, None] * inv[None, :]
    ang_c = col.astype(F32)[:, None] * inv[None, :]
    zero = jnp.zeros_like(ang_r)
    cos64 = jnp.concatenate([jnp.cos(ang_r), jnp.cos(ang_r), jnp.cos(ang_c), jnp.cos(ang_c)], axis=1)
    s1_64 = jnp.concatenate([-jnp.sin(ang_r), zero, -jnp.sin(ang_c), zero], axis=1)
    s2_64 = jnp.concatenate([zero, jnp.sin(ang_r), zero, jnp.sin(ang_c)], axis=1)
    rep = LANES // HEAD_DIM
    cos_t, s1_t, s2_t = (jnp.tile(t, (1, rep)) for t in (cos64, s1_64, s2_64))
    nseq = seq // tm
    tab_spec = pl.BlockSpec((tm, LANES), lambda i: (i % nseq, 0))
    full = lambda shape: pl.BlockSpec(shape, lambda i: (0,) * len(shape))
    kern = functools.partial(_even_in_kernel, qk_w=qk_w, q_w=q_w, kv_w=kv_w, b_w=b_w)
    return pl.pallas_call(
        kern,
        grid=(n // tm,),
        in_specs=[pl.BlockSpec((tm, d), lambda i: (i, 0)), full((1, d)), full(w_in.shape),
                  full((1, qk_w)), full((qk_w, qk_w)), tab_spec, tab_spec, tab_spec],
        out_specs=[_chunkT_spec(q_w, tm, nseq), pl.BlockSpec((tm, kv_w), lambda i: (i, 0)),
                   _chunkT_spec(kv_w, tm, nseq), pl.BlockSpec((tm, b_w), lambda i: (i, 0))],
        out_shape=[jax.ShapeDtypeStruct((n // seq, nseq, q_w, tm), BF16), jax.ShapeDtypeStruct((n, kv_w), BF16),
                   jax.ShapeDtypeStruct((n // seq, nseq, kv_w, tm), BF16), jax.ShapeDtypeStruct((n, b_w), F32)],
        compiler_params=_cparams("parallel"),
        name="even_in_proj",
    )(x2d, g[None, :].astype(F32), w_in.astype(BF16), gqk, bd, cos_t, s1_t, s2_t)


def _flash_pipelined(nk, n_lead, n_maps, score_fn, shift_fn, value_fn, s_a, s_b, acc_ref, tq, dv):
    assert nk % 2 == 0 and n_lead % 2 == 0 and nk - n_lead >= 2
    ones = jnp.ones((BF16_ROWS, s_a.shape[1]), BF16)

    def scores(t, s_ref, lead):
        shift = shift_fn(t, lead)
        cms = []
        for c in range(n_maps):
            s = score_fn(t, c, lead)
            s_ref[c] = s
            cm = jnp.max(s, axis=0, keepdims=True)
            cms.append(cm if shift is None else cm + shift)
        return tuple(cms)

    def softmax(t, s_ref, cms, state, lead):
        shift = shift_fn(t, lead)
        vals = jnp.concatenate([value_fn(t), ones], axis=0)
        new = []
        for c in range(n_maps):
            m = state[c]
            m_new = jnp.maximum(m, cms[c])
            alpha = jnp.exp2(m - m_new)
            p = jnp.exp2(s_ref[c] - (m_new if shift is None else m_new - shift))
            acc_ref[c] = alpha * acc_ref[c] + jnp.dot(vals, p.astype(BF16), preferred_element_type=F32)
            new.append(m_new)
        return tuple(new)

    acc_ref[...] = jnp.zeros(acc_ref.shape, F32)
    state = tuple(jnp.full((1, tq), NEG_BIG, F32) for _ in range(n_maps))
    bufs = (s_a, s_b)
    cms = scores(0, s_a, n_lead > 0)
    for t in range(n_lead):
        nxt = scores(t + 1, bufs[(t + 1) % 2], t + 1 < n_lead)
        state = softmax(t, bufs[t % 2], cms, state, True)
        cms = nxt

    def pair(u, carry):
        cms_a, state = carry
        t = n_lead + 2 * u
        cms_b = scores(t + 1, s_b, False)
        state = softmax(t, s_a, cms_a, state, False)
        cms_a = scores(t + 2, s_a, False)
        state = softmax(t + 1, s_b, cms_b, state, False)
        return cms_a, state

    cms_a, state = lax.fori_loop(0, (nk - n_lead) // 2 - 1, pair, (cms, state))
    cms_b = scores(nk - 1, s_b, False)
    state = softmax(nk - 2, s_a, cms_a, state, False)
    state = softmax(nk - 1, s_b, cms_b, state, False)
    return tuple(acc_ref[c, :dv, :] / acc_ref[c, dv:dv + 1, :] for c in range(n_maps))


def _gqa_kernel(qT_ref, k_ref, vT_ref, o_ref, s_a, s_b, acc_ref, *, nk, tk, nh):
    hd = HEAD_DIM
    tq = qT_ref.shape[3]
    kv = pl.program_id(1) // (A_GROUP // nh)
    row_head = lax.broadcasted_iota(jnp.int32, (A_KV_HEADS * hd, tq), 0) // hd
    qs = []
    for h in range(nh):
        qh = qT_ref[0, 0, h * hd:(h + 1) * hd, :]
        qs.append(jnp.where(row_head == kv, jnp.concatenate([qh] * A_KV_HEADS, axis=0), jnp.zeros((), BF16)))

    def score_fn(t, c, lead):
        kc = k_ref[0, pl.ds(pl.multiple_of(t * tk, tk), tk), :]
        return jnp.dot(kc, qs[c], preferred_element_type=F32)

    res = _flash_pipelined(nk, 0, nh, score_fn, lambda t, lead: None, lambda t: vT_ref[0, t], s_a, s_b, acc_ref,
                           tq, hd)
    for h, o in enumerate(res):
        o_ref[0, 0, h * hd:(h + 1) * hd, :] = o.astype(o_ref.dtype)


def _gqa_attention(qT, k, vT, batch, seq):
    hd, nh = HEAD_DIM, HEADS_PER_STEP_A
    tq = qT.shape[3]
    tk = vT.shape[3]
    nk = seq // tk
    hsteps = A_HEADS // nh
    per_kv = A_GROUP // nh
    k3 = k.reshape(batch, seq, A_KV_HEADS * hd)
    return pl.pallas_call(
        functools.partial(_gqa_kernel, nk=nk, tk=tk, nh=nh),
        grid=(batch, hsteps, seq // tq),
        in_specs=[pl.BlockSpec((1, 1, nh * hd, tq), lambda b, h, i: (b, i, h, 0)),
                  pl.BlockSpec((1, seq, A_KV_HEADS * hd), lambda b, h, i: (b, 0, 0)),
                  pl.BlockSpec((1, nk, hd, tk), lambda b, h, i: (b, 0, h // per_kv, 0))],
        out_specs=pl.BlockSpec((1, 1, nh * hd, tq), lambda b, h, i: (b, i, h, 0)),
        out_shape=jax.ShapeDtypeStruct(qT.shape, BF16),
        scratch_shapes=[pltpu.VMEM((nh, tk, tq), F32), pltpu.VMEM((nh, tk, tq), F32),
                        pltpu.VMEM((nh, hd + BF16_ROWS, tq), F32)],
        compiler_params=_cparams("parallel", "parallel", "parallel"),
        name="gqa_attention",
    )(qT, k3, vT)


def _conv_kernel(prev_ref, cur_ref, next_ref, w_ref, b_ref, g_ref, beta_ref, bd_ref, o_ref, xe_ref, xs_ref, *, ts):
    i = pl.program_id(1)
    last = pl.num_programs(1) - 1
    halo = CONV_HALO
    xe_ref[0:halo, :] = jnp.where(i > 0, prev_ref[0], 0.0)
    xe_ref[halo:halo + ts, :] = cur_ref[0]
    xe_ref[halo + ts:halo + ts + halo, :] = jnp.where(i < last, next_ref[0], 0.0)
    rows = xs_ref.shape[1]
    for rho in range(1, SUBLANES):
        xs_ref[rho - 1] = xe_ref[rho:rho + rows, :]
    w = w_ref[...]
    first = halo - CONV_WIDTH // 2
    for r0 in range(0, ts, CONV_SUB):
        acc = jnp.zeros((CONV_SUB, w.shape[1]), F32) + b_ref[...]
        for j in range(CONV_WIDTH):
            rho = (first + j) % SUBLANES
            base = r0 + first + j - rho
            src = xe_ref[base:base + CONV_SUB, :] if rho == 0 else xs_ref[rho - 1, base:base + CONV_SUB, :]
            acc = acc + w[j:j + 1, :] * src
        mu = _split_dot(acc, bd_ref[...])
        cen = acc - mu
        var = _split_dot(cen * cen, bd_ref[...])
        hn = cen * lax.rsqrt(var + 1e-5) * g_ref[...] + beta_ref[...]
        o_ref[0, r0:r0 + CONV_SUB, :] = (hn * _sigmoid(hn)).astype(o_ref.dtype)


def _conformer_conv(glu, conv_w, conv_b, gn_g, gn_b, batch, seq):
    c = glu.shape[1]
    ts, halo = TS_CONV, CONV_HALO
    x3 = glu.reshape(batch, seq, c)
    hpb = ts // halo
    nh = seq // halo
    gid = jnp.arange(c) // B_GROUP_WIDTH
    bd = (gid[:, None] == gid[None, :]).astype(BF16) * (1.0 / B_GROUP_WIDTH)
    wpad = jnp.zeros((32, c), F32).at[:CONV_WIDTH].set(conv_w.astype(F32))
    vec = lambda a: a[None, :].astype(F32)
    full = lambda shape: pl.BlockSpec(shape, lambda b, i: (0,) * len(shape))
    out = pl.pallas_call(
        functools.partial(_conv_kernel, ts=ts),
        grid=(batch, seq // ts),
        in_specs=[pl.BlockSpec((1, halo, c), lambda b, i: (b, jnp.maximum(i * hpb - 1, 0), 0)),
                  pl.BlockSpec((1, ts, c), lambda b, i: (b, i, 0)),
                  pl.BlockSpec((1, halo, c), lambda b, i: (b, jnp.minimum((i + 1) * hpb, nh - 1), 0)),
                  full((32, c)), full((1, c)), full((1, c)), full((1, c)), full((c, c))],
        out_specs=pl.BlockSpec((1, ts, c), lambda b, i: (b, i, 0)),
        out_shape=jax.ShapeDtypeStruct((batch, seq, c), BF16),
        scratch_shapes=[pltpu.VMEM((ts + 2 * halo, c), F32),
                        pltpu.VMEM((SUBLANES - 1, ts + 2 * halo - SUBLANES, c), F32)],
        compiler_params=_cparams("parallel", "parallel"),
        name="conformer_conv",
    )(x3, x3, x3, wpad, vec(conv_b), vec(gn_g), vec(gn_b), bd)
    return out.reshape(batch * seq, c)


def _route(logits):
    lane = lax.broadcasted_iota(jnp.int32, logits.shape, 1)
    big = jnp.int32(1 << 20)
    ninf = jnp.float32(-jnp.inf)
    gmask = (lane >= MOE_EXPERTS) & (lane < MOE_EXPERTS + MOE_GROUPS)
    lg = jnp.where(gmask, logits, ninf)
    mg = jnp.max(lg, axis=-1, keepdims=True)
    sg = jnp.sum(jnp.exp(lg - mg), axis=-1, keepdims=True)
    g_val = 1.0 / sg
    g_idx = jnp.min(jnp.where(lg == mg, lane - MOE_EXPERTS, big), axis=-1, keepdims=True)
    emask = (lane < MOE_EXPERTS) & ((lane // MOE_PER_GROUP) == g_idx)
    le = jnp.where(emask, logits, ninf)
    m1 = jnp.max(le, axis=-1, keepdims=True)
    se = jnp.sum(jnp.exp(le - m1), axis=-1, keepdims=True)
    i1 = jnp.min(jnp.where(le == m1, lane, big), axis=-1, keepdims=True)
    le2 = jnp.where(lane == i1, ninf, le)
    m2 = jnp.max(le2, axis=-1, keepdims=True)
    i2 = jnp.min(jnp.where(le2 == m2, lane, big), axis=-1, keepdims=True)
    e1 = 1.0 / se
    e2 = jnp.exp(m2 - m1) / se
    norm = g_val / (e1 + e2)
    ids = jnp.where(lane == 0, i1, i2).astype(F32)
    gates = jnp.where(lane == MOE_TOPK, e1 * norm, e2 * norm)
    return jnp.where(lane < MOE_TOPK, ids, gates)


def _out_router_kernel(*refs, has_yb):
    if has_yb:
        yT_ref, yb_ref, x_ref, wa_ref, wb_ref = refs[:5]
        rest = refs[5:]
    else:
        yT_ref, x_ref, wa_ref = refs[:3]
        rest = refs[3:]
    g_ref, wrh_ref, wrl_ref, br_ref, x1_ref, h_ref, route_ref = rest
    ya = yT_ref[0, 0].astype(F32).T.astype(BF16)
    x1 = x_ref[...] + jnp.dot(ya, wa_ref[...], preferred_element_type=F32)
    if has_yb:
        x1 = x1 + jnp.dot(yb_ref[...], wb_ref[...], preferred_element_type=F32)
    x1_ref[...] = x1
    h = _rms_rows(x1, g_ref[...], EPS)
    _rows_to_tiles(h_ref, h)
    hi = h.astype(BF16)
    lo = (h - hi.astype(F32)).astype(BF16)
    logits = (jnp.dot(hi, wrh_ref[...], preferred_element_type=F32)
              + jnp.dot(hi, wrl_ref[...], preferred_element_type=F32)
              + jnp.dot(lo, wrh_ref[...], preferred_element_type=F32) + br_ref[...])
    route_ref[...] = _route(logits).T[:SUBLANES]


def _out_router(yT, yb, x2d, w_out, g_ffn, w_group, b_group, w_expert, b_expert):
    n, d = x2d.shape
    tm = yT.shape[3]
    nseq = yT.shape[1]
    wa = yT.shape[2]
    wb16 = w_out.astype(BF16)
    wr = jnp.zeros((d, LANES), F32).at[:, :MOE_EXPERTS].set(w_expert.astype(F32))
    wr = wr.at[:, MOE_EXPERTS:MOE_EXPERTS + MOE_GROUPS].set(w_group.astype(F32))
    wrh = wr.astype(BF16)
    wrl = (wr - wrh.astype(F32)).astype(BF16)
    br = jnp.zeros((1, LANES), F32).at[0, :MOE_EXPERTS].set(b_expert.astype(F32))
    br = br.at[0, MOE_EXPERTS:MOE_EXPERTS + MOE_GROUPS].set(b_group.astype(F32))
    full = lambda shape: pl.BlockSpec(shape, lambda i: (0,) * len(shape))
    row = lambda w: pl.BlockSpec((tm, w), lambda i: (i, 0))
    has_yb = yb is not None
    if has_yb:
        args = [yT, yb, x2d, wb16[:wa], wb16[wa:]]
        specs = [_chunkT_spec(wa, tm, nseq), row(yb.shape[1]), row(d), full((wa, d)), full((w_out.shape[0] - wa, d))]
    else:
        assert wa == w_out.shape[0]
        args = [yT, x2d, wb16]
        specs = [_chunkT_spec(wa, tm, nseq), row(d), full((wa, d))]
    return pl.pallas_call(
        functools.partial(_out_router_kernel, has_yb=has_yb),
        grid=(n // tm,),
        in_specs=specs + [full((1, d)), full((d, LANES)), full((d, LANES)), full((1, LANES))],
        out_specs=[row(d), pl.BlockSpec((tm * SUBLANES, LANES), lambda i: (i, 0)),
                   pl.BlockSpec((SUBLANES, tm), lambda i: (0, i))],
        out_shape=[jax.ShapeDtypeStruct((n, d), F32), jax.ShapeDtypeStruct((n * SUBLANES, LANES), F32),
                   jax.ShapeDtypeStruct((SUBLANES, n), F32)],
        compiler_params=_cparams("parallel"),
        name="out_proj_router",
    )(*args, g_ffn[None, :].astype(F32), wrh, wrl, br)


def _moe_dispatch(route, n):
    ids = route[:MOE_TOPK].astype(jnp.int32)
    gates = route[MOE_TOPK:2 * MOE_TOPK]
    bm = MOE_BM
    a = n * MOE_TOPK
    p = a + MOE_EXPERTS * bm
    flat_e = ids.reshape(-1)
    flat_w = gates.reshape(-1)
    order = jnp.argsort(flat_e).astype(jnp.int32)
    experts = jnp.arange(MOE_EXPERTS, dtype=jnp.int32)
    counts = jnp.sum((flat_e[:, None] == experts[None, :]).astype(jnp.int32), axis=0)
    pcounts = (counts + bm - 1) // bm * bm
    starts = jnp.cumsum(counts) - counts
    pends = jnp.cumsum(pcounts)
    pstarts = pends - pcounts
    n_blk = p // bm
    blk_start = jnp.arange(n_blk, dtype=jnp.int32) * bm
    blk_e = jnp.minimum(jnp.sum((blk_start[:, None] >= pends[None, :]).astype(jnp.int32), axis=1),
                        MOE_EXPERTS - 1)
    off = (blk_start - pstarts[blk_e])[:, None] + jnp.arange(bm, dtype=jnp.int32)[None, :]
    valid = (off < counts[blk_e][:, None]).reshape(p)
    sidx = jnp.clip(starts[blk_e][:, None] + off, 0, a - 1).reshape(p)
    assign = order[sidx]
    tok = assign % n
    slot = assign // n
    src_tok = jnp.where(valid, tok, 0)
    dump = a + jnp.cumsum(1 - valid.astype(jnp.int32)) - 1
    dst_row = jnp.where(valid, slot * n + tok, dump).astype(jnp.int32)
    row_w = jnp.where(valid, flat_w[assign], 0.0)
    tok_bits = (n - 1).bit_length()
    src_tok = jnp.concatenate([jnp.zeros((bm,), jnp.int32), src_tok])
    dst_row = jnp.concatenate([p + jnp.arange(bm, dtype=jnp.int32), dst_row])
    assert tok_bits + (p + 2 * bm - 1).bit_length() <= 32
    packed = lax.bitcast_convert_type(
        src_tok.astype(jnp.uint32) | (dst_row.astype(jnp.uint32) << tok_bits), jnp.int32)
    return blk_e, packed, row_w[:, None], tok_bits


def _moe_kernel(be_ref, idx_ref, h_hbm, w1_ref, w3_ref, w2_ref, rw_ref, y_hbm,
                xbuf, obuf, gsem, ssem, *, tok_bits, spare_row):
    del be_ref
    bm = MOE_BM
    i = pl.program_id(0)
    nb = pl.num_programs(0)
    slot = i % 2
    tok_mask = (1 << tok_bits) - 1

    tile = lambda r: pl.ds(pl.multiple_of(r * SUBLANES, SUBLANES), SUBLANES)

    def gather_row(blk, sl, r):
        t = idx_ref[(blk + 1) * bm + r] & tok_mask
        pltpu.make_async_copy(h_hbm.at[tile(t)], xbuf.at[sl, tile(r)], gsem.at[sl]).start()

    def scatter_row(blk, sl, r):
        t = lax.shift_right_logical(idx_ref[(blk + 1) * bm + r], tok_bits)
        pltpu.make_async_copy(obuf.at[sl, tile(r)], y_hbm.at[tile(t)], ssem.at[sl]).start()

    def row_loop(body):
        lax.fori_loop(0, bm, lambda r, c: (body(r), c)[1], 0, unroll=MOE_DMA_UNROLL)

    def gather_wait(sl):
        pltpu.make_async_copy(h_hbm.at[pl.ds(0, bm * SUBLANES)], xbuf.at[sl], gsem.at[sl]).wait()

    def scatter_wait(sl):
        pltpu.make_async_copy(obuf.at[sl], y_hbm.at[pl.ds(0, bm * SUBLANES)], ssem.at[sl]).wait()

    @pl.when(i == 0)
    def _():
        obuf[...] = jnp.zeros(obuf.shape, obuf.dtype)
        row_loop(lambda r: gather_row(0, 0, r))
        row_loop(lambda r: pltpu.make_async_copy(obuf.at[0, tile(r)], y_hbm.at[tile(spare_row + r)],
                                                 ssem.at[0]).start())

    gather_wait(slot)
    nxt = jnp.minimum(i + 1, nb - 1)
    for r in range(bm):
        gather_row(nxt, 1 - slot, r)
        scatter_row(i - 1, 1 - slot, r)
    x = _rows_from_tiles(xbuf.at[slot]).astype(BF16)
    h1 = jnp.dot(x, w1_ref[0, 0].astype(BF16), preferred_element_type=F32)
    h3 = jnp.dot(x, w3_ref[0, 0].astype(BF16), preferred_element_type=F32)
    hm = (h1 * _sigmoid(h1) * h3).astype(BF16)
    y = jnp.dot(hm, w2_ref[0, 0].astype(BF16), preferred_element_type=F32) * rw_ref[...]
    scatter_wait(slot)
    _rows_to_tiles(obuf.at[slot], y)

    @pl.when(i == nb - 1)
    def _():
        row_loop(lambda r: scatter_row(i, slot, r))
        scatter_wait(slot)
        scatter_wait(1 - slot)
        gather_wait(1 - slot)


def _moe_ffn(h_tiles, route, w1, w3, w2, layer):
    d = SUBLANES * LANES
    n = h_tiles.shape[0] // SUBLANES
    assert w1.shape[2] == d and h_tiles.shape[1] == LANES
    bm = MOE_BM
    hid = w1.shape[-1]
    blk_e, packed, row_w, tok_bits = _moe_dispatch(route, n)
    n_blk = blk_e.shape[0]
    rows_real = n * MOE_TOPK + MOE_EXPERTS * bm
    rows_out = rows_real + 2 * bm
    grid_spec = pltpu.PrefetchScalarGridSpec(
        num_scalar_prefetch=2,
        grid=(n_blk,),
        in_specs=[pl.BlockSpec(memory_space=pl.ANY),
                  pl.BlockSpec((1, 1, d, hid), lambda i, be, ix: (layer, be[i], 0, 0)),
                  pl.BlockSpec((1, 1, d, hid), lambda i, be, ix: (layer, be[i], 0, 0)),
                  pl.BlockSpec((1, 1, hid, d), lambda i, be, ix: (layer, be[i], 0, 0)),
                  pl.BlockSpec((bm, 1), lambda i, be, ix: (i, 0))],
        out_specs=pl.BlockSpec(memory_space=pl.ANY),
        scratch_shapes=[pltpu.VMEM((2, bm * SUBLANES, LANES), F32), pltpu.VMEM((2, bm * SUBLANES, LANES), F32),
                        pltpu.SemaphoreType.DMA((2,)), pltpu.SemaphoreType.DMA((2,))],
    )
    return pl.pallas_call(
        functools.partial(_moe_kernel, tok_bits=tok_bits, spare_row=rows_real + bm),
        grid_spec=grid_spec,
        out_shape=jax.ShapeDtypeStruct((rows_out * SUBLANES, LANES), F32),
        compiler_params=_cparams("arbitrary"),
        name="moe_ffn",
    )(blk_e, packed, h_tiles, w1, w3, w2, row_w)


def _odd_in_kernel(x_ref, y0_ref, y1_ref, g_ref, w_ref, xo_ref, q_ref, k_ref, v_ref, *, cw):
    x = x_ref[...] + _rows_from_tiles(y0_ref) + _rows_from_tiles(y1_ref)
    xo_ref[...] = x
    h = _rms_rows(x, g_ref[...], EPS)
    z = jnp.dot(h.astype(BF16), w_ref[...], preferred_element_type=F32)
    q_ref[0, 0] = (z[:, :cw] * (C_HEAD_DIM ** -0.5 * LOG2E)).T.astype(BF16)
    k_ref[...] = z[:, cw:2 * cw].astype(BF16)
    v_ref[0, 0] = z[:, 2 * cw:3 * cw].T.astype(BF16)


def _odd_in(x2d, ybuf, g, w_in, seq):
    n, d = x2d.shape
    tm = TM_PROJ
    cw = w_in.shape[1] // 3
    nt = n // tm
    row = lambda w: pl.BlockSpec((tm, w), lambda i: (i, 0))
    full = lambda shape: pl.BlockSpec(shape, lambda i: (0,) * len(shape))
    return pl.pallas_call(
        functools.partial(_odd_in_kernel, cw=cw),
        grid=(nt,),
        in_specs=[row(d), pl.BlockSpec((tm * SUBLANES, LANES), lambda i: (i, 0)),
                  pl.BlockSpec((tm * SUBLANES, LANES), lambda i: (i + nt, 0)), full((1, d)), full(w_in.shape)],
        out_specs=[row(d), _chunkT_spec(cw, tm, seq // tm), row(cw), _chunkT_spec(cw, tm, seq // tm)],
        out_shape=[jax.ShapeDtypeStruct((n, d), F32), jax.ShapeDtypeStruct((n // seq, seq // tm, cw, tm), BF16),
                   jax.ShapeDtypeStruct((n, cw), BF16), jax.ShapeDtypeStruct((n // seq, seq // tm, cw, tm), BF16)],
        compiler_params=_cparams("parallel"),
        name="odd_in_proj",
    )(x2d, ybuf, ybuf, g[None, :].astype(F32), w_in.astype(BF16))


def _t5_bucket(rel):
    nb = REL_BUCKETS // 2
    max_exact = nb // 2
    ret = jnp.where(rel > 0, nb, 0).astype(jnp.int32)
    n = jnp.abs(rel)
    nf = jnp.maximum(n, 1).astype(F32)
    large = max_exact + (jnp.log(nf / max_exact) / math.log(REL_MAX_DIST / max_exact) * (nb - max_exact)).astype(jnp.int32)
    large = jnp.minimum(large, nb - 1)
    return ret + jnp.where(n < max_exact, n, large)


def _diff_kernel(cfar_ref, lam_ref, qT_ref, k_ref, vT_ref, band_ref, gsub_ref, o_ref, s_a, s_b, acc_ref, *, nk, tk):
    head = pl.program_id(1)
    i = pl.program_id(2)
    hd = C_HEAD_DIM
    tq = qT_ref.shape[3]
    dv = vT_ref.shape[2]
    qT = qT_ref[0, 0]
    row_map = lax.broadcasted_iota(jnp.int32, qT.shape, 0) // hd
    qs = [jnp.where(row_map == c, qT, jnp.zeros((), BF16)) for c in range(2)]

    chunk = lambda t: (i + (nk - 1) + t) % nk

    def score_fn(t, c, lead):
        j = chunk(t)
        s = jnp.dot(k_ref[0, pl.ds(pl.multiple_of(j * tk, tk), tk), :], qs[c], preferred_element_type=F32)
        if lead:
            s = s + band_ref[0, jnp.clip(j - i, -2, 2) + 2]
        return s

    def shift_fn(t, lead):
        if lead:
            return None
        return jnp.where(chunk(t) < i, cfar_ref[head, 0], cfar_ref[head, 1])

    o0, o1 = _flash_pipelined(nk, DIFF_LEAD, 2, score_fn, shift_fn, lambda t: vT_ref[0, chunk(t)],
                              s_a, s_b, acc_ref, tq, dv)
    o = o0 - lam_ref[...] * o1
    ms = jnp.mean(o * o, axis=0, keepdims=True)
    o_ref[0, 0] = (o * lax.rsqrt(ms + 1e-5) * gsub_ref[...]).astype(o_ref.dtype)


def _bias_band(rel_bias, t):
    assert t >= REL_MAX_DIST
    rb = rel_bias.astype(F32) * LOG2E
    period = 2 * t
    m = jnp.arange(period, dtype=jnp.int32)
    d3 = jnp.arange(-1, 2, dtype=jnp.int32)[:, None] * t
    rel = jnp.where(m[None, :] < t, d3 - m[None, :], d3 + (period - m[None, :]))
    vec = rb[_t5_bucket(rel)].transpose(2, 0, 1)
    nh = vec.shape[0]
    skew = jnp.tile(vec, (1, 1, t))[:, :, :t * (period - 1)].reshape(nh, 3, t, period - 1)[..., :t]
    far = lambda r: jnp.broadcast_to(rb[_t5_bucket(jnp.int32(r))][:, None, None, None], (nh, 1, t, t))
    return jnp.concatenate([far(-2 * t), skew, far(2 * t)], axis=1)


def _diff_attention(qT, k, vT, lam, lam_init, subln, rel_bias, batch, seq):
    hd = C_HEAD_DIM
    dv = 2 * hd
    tq = qT.shape[3]
    tk = vT.shape[3]
    nk = seq // tk
    assert tk == tq
    k3 = k.reshape(batch, seq, C_HEADS * dv)
    assert nk >= DIFF_LEAD + 2
    band = _bias_band(rel_bias, tk)
    rb = rel_bias.astype(F32) * LOG2E
    cfar = jnp.stack([rb[_t5_bucket(jnp.int32(-2 * tk))], rb[_t5_bucket(jnp.int32(2 * tk))]], axis=1)
    lam_vec = jnp.full((1, tq), lam, F32)
    gsub = (subln.astype(F32) * (1.0 - lam_init))[:, None]
    return pl.pallas_call(
        functools.partial(_diff_kernel, nk=nk, tk=tk),
        grid=(batch, C_HEADS, seq // tq),
        in_specs=[pl.BlockSpec(memory_space=pltpu.SMEM),
                  pl.BlockSpec((1, tq), lambda b, h, i: (0, 0)),
                  pl.BlockSpec((1, 1, dv, tq), lambda b, h, i: (b, i, h, 0)),
                  pl.BlockSpec((1, seq, dv), lambda b, h, i: (b, 0, h)),
                  pl.BlockSpec((1, nk, dv, tk), lambda b, h, i: (b, 0, h, 0)),
                  pl.BlockSpec((1, 5, tk, tq), lambda b, h, i: (h, 0, 0, 0)),
                  pl.BlockSpec((dv, 1), lambda b, h, i: (0, 0))],
        out_specs=pl.BlockSpec((1, 1, dv, tq), lambda b, h, i: (b, i, h, 0)),
        out_shape=jax.ShapeDtypeStruct(qT.shape, BF16),
        scratch_shapes=[pltpu.VMEM((2, tk, tq), F32), pltpu.VMEM((2, tk, tq), F32),
                        pltpu.VMEM((2, dv + BF16_ROWS, tq), F32)],
        compiler_params=_cparams("parallel", "parallel", "parallel"),
        name="diff_attention",
    )(cfar, lam_vec, qT, k3, vT, band, gsub)


def _final_kernel(x_ref, y0_ref, y1_ref, g_ref, o_ref):
    x = x_ref[...] + _rows_from_tiles(y0_ref) + _rows_from_tiles(y1_ref)
    o_ref[...] = _rms_rows(x, g_ref[...], EPS)


def _final(x2d, ybuf, g):
    n, d = x2d.shape
    tm = TM_PROJ
    nt = n // tm
    row = pl.BlockSpec((tm, d), lambda i: (i, 0))
    return pl.pallas_call(
        _final_kernel,
        grid=(nt,),
        in_specs=[row, pl.BlockSpec((tm * SUBLANES, LANES), lambda i: (i, 0)),
                  pl.BlockSpec((tm * SUBLANES, LANES), lambda i: (i + nt, 0)), pl.BlockSpec((1, d), lambda i: (0, 0))],
        out_specs=row,
        out_shape=jax.ShapeDtypeStruct((n, d), F32),
        compiler_params=_cparams("parallel"),
        name="final_norm",
    )(x2d, ybuf, ybuf, g[None, :].astype(F32))


def kernel(x, norm_mix, norm_ffn, norm_final, ev_w_in, ev_q_norm, ev_k_norm, ev_conv_w, ev_conv_b, ev_gn_g, ev_gn_b, ev_w_out, od_w_in, od_lam_q1, od_lam_k1, od_lam_q2, od_lam_k2, od_subln, od_w_out, rel_bias, moe_w_group, moe_b_group, moe_w_expert, moe_b_expert, moe_w1, moe_w3, moe_w2):
    batch, seq, d = x.shape
    n = batch * seq
    x2d = x.reshape(n, d).astype(F32)

    qT, k, vT, glu = _even_in(x2d, norm_mix[0], ev_w_in[0], ev_q_norm[0], ev_k_norm[0], seq)
    yaT = _gqa_attention(qT, k, vT, batch, seq)
    yb = _conformer_conv(glu, ev_conv_w[0], ev_conv_b[0], ev_gn_g[0], ev_gn_b[0], batch, seq)
    x1, h, route = _out_router(yaT, yb, x2d, ev_w_out[0], norm_ffn[0],
                                    moe_w_group[0], moe_b_group[0], moe_w_expert[0], moe_b_expert[0])
    ybuf = _moe_ffn(h, route, moe_w1, moe_w3, moe_w2, 0)

    x2, qT, k, vT = _odd_in(x1, ybuf, norm_mix[1], od_w_in[0], seq)
    layer_idx = 1
    lam_init = 0.8 - 0.6 * math.exp(-0.3 * layer_idx)
    lam = (jnp.exp(jnp.sum(od_lam_q1[0].astype(F32) * od_lam_k1[0].astype(F32)))
           - jnp.exp(jnp.sum(od_lam_q2[0].astype(F32) * od_lam_k2[0].astype(F32))) + lam_init)
    oT = _diff_attention(qT, k, vT, lam, lam_init, od_subln[0], rel_bias, batch, seq)
    x3, h, route = _out_router(oT, None, x2, od_w_out[0], norm_ffn[1],
                                    moe_w_group[1], moe_b_group[1], moe_w_expert[1], moe_b_expert[1])
    ybuf = _moe_ffn(h, route, moe_w1, moe_w3, moe_w2, 1)
    out = _final(x3, ybuf, norm_final)
    return out.reshape(batch, seq, d)
```

```python
import functools
import math

import jax
import jax.numpy as jnp
from jax import lax
from jax.experimental import pallas as pl
from jax.experimental.pallas import tpu as pltpu

F32 = jnp.float32
BF16 = jnp.bfloat16

GRID_W = 64
HEAD_DIM = 64
EPS = 1e-6
A_HEADS = 8
A_KV_HEADS = 2
A_GROUP = A_HEADS // A_KV_HEADS
ROPE_THETA = 10000.0
ROPE_HALF = HEAD_DIM // 4
B_GROUP_WIDTH = 64
CONV_WIDTH = 31
CONV_HALO = 16
C_HEADS = 8
C_HEAD_DIM = 64
REL_BUCKETS = 32
REL_MAX_DIST = 128
MOE_GROUPS = 4
MOE_PER_GROUP = 8
MOE_EXPERTS = MOE_GROUPS * MOE_PER_GROUP
MOE_TOPK = 2
LOG2E = 1.4426950408889634
NEG_BIG = -1e30
LANES = 128
SUBLANES = 8
MOE_DMA_UNROLL = 8

TM_PROJ = 512
HEADS_PER_STEP_A = 2
DIFF_LEAD = 4
BF16_ROWS = 16
TS_CONV = 256
CONV_SUB = 64
MOE_BM = 128
VMEM_LIMIT = 56 * 1024 * 1024


def _cparams(*sem):
    return pltpu.CompilerParams(dimension_semantics=sem, vmem_limit_bytes=VMEM_LIMIT)


def _split_dot(a, b_bf16):
    hi = a.astype(BF16)
    lo = (a - hi.astype(F32)).astype(BF16)
    return (jnp.dot(hi, b_bf16, preferred_element_type=F32)
            + jnp.dot(lo, b_bf16, preferred_element_type=F32))


def _rms_rows(x, g, eps):
    ms = jnp.mean(x * x, axis=-1, keepdims=True)
    return x * lax.rsqrt(ms + eps) * g


def _sigmoid(x):
    return 1.0 / (1.0 + jnp.exp(-x))


def _chunkT_spec(width, tm, nseq):
    return pl.BlockSpec((1, 1, width, tm), lambda i: (i // nseq, i % nseq, 0, 0))


def _rows_from_tiles(ref):
    rows = ref.shape[0] // SUBLANES
    return jnp.concatenate([ref[pl.ds(s, rows, stride=SUBLANES), :] for s in range(SUBLANES)], axis=1)


def _rows_to_tiles(ref, val):
    rows = val.shape[0]
    for s in range(SUBLANES):
        ref[pl.ds(s, rows, stride=SUBLANES), :] = val[:, s * LANES:(s + 1) * LANES]


def _even_in_kernel(x_ref, g_ref, w_ref, gqk_ref, bd_ref, cos_ref, s1_ref, s2_ref,
                    q_ref, k_ref, v_ref, glu_ref, *, qk_w, q_w, kv_w, b_w):
    h = _rms_rows(x_ref[...], g_ref[...], EPS)
    z = jnp.dot(h.astype(BF16), w_ref[...], preferred_element_type=F32)
    qk = z[:, :qk_w]
    ms = _split_dot(qk * qk, bd_ref[...])
    qkn = qk * lax.rsqrt(ms + EPS) * gqk_ref[...]
    cos, s1, s2 = cos_ref[...], s1_ref[...], s2_ref[...]
    parts = []
    for c in range(qk_w // LANES):
        blk = qkn[:, c * LANES:(c + 1) * LANES]
        up = pltpu.roll(blk, LANES - ROPE_HALF, 1)
        dn = pltpu.roll(blk, ROPE_HALF, 1)
        parts.append(blk * cos + up * s1 + dn * s2)
    rot = jnp.concatenate(parts, axis=1)
    q_ref[0, 0] = rot[:, :q_w].T.astype(BF16)
    k_ref[...] = rot[:, q_w:qk_w].astype(BF16)
    v_ref[0, 0] = z[:, qk_w:qk_w + kv_w].T.astype(BF16)
    a = z[:, qk_w + kv_w:qk_w + kv_w + b_w]
    gate = z[:, qk_w + kv_w + b_w:qk_w + kv_w + 2 * b_w]
    glu_ref[...] = a * _sigmoid(gate)


def _even_in(x2d, g, w_in, q_norm, k_norm, seq):
    n, d = x2d.shape
    q_w, kv_w, b_w = A_HEADS * HEAD_DIM, A_KV_HEADS * HEAD_DIM, d // 2
    qk_w = q_w + kv_w
    tm = TM_PROJ
    gq = jnp.tile(q_norm.astype(F32), A_HEADS) * (HEAD_DIM ** -0.5 * LOG2E)
    gk = jnp.tile(k_norm.astype(F32), A_KV_HEADS)
    gqk = jnp.concatenate([gq, gk])[None, :]
    hid = jnp.arange(qk_w) // HEAD_DIM
    bd = (hid[:, None] == hid[None, :]).astype(BF16) * (1.0 / HEAD_DIM)
    pos = jnp.arange(seq, dtype=jnp.int32)
    row, col = pos // GRID_W, pos % GRID_W
    m = ROPE_HALF
    inv = ROPE_THETA ** (-jnp.arange(m, dtype=F32) / m)
    ang_r = row.astype(F32)[:, None] * inv[None, :]
    ang_c = col.astype(F32)[:, None] * inv[None, :]
    zero = jnp.zeros_like(ang_r)
    cos64 = jnp.concatenate([jnp.cos(ang_r), jnp.cos(ang_r), jnp.cos(ang_c), jnp.cos(ang_c)], axis=1)
    s1_64 = jnp.concatenate([-jnp.sin(ang_r), zero, -jnp.sin(ang_c), zero], axis=1)
    s2_64 = jnp.concatenate([zero, jnp.sin(ang_r), zero, jnp.sin(ang_c)], axis=1)
    rep = LANES // HEAD_DIM
    cos_t, s1_t, s2_t = (jnp.tile(t, (1, rep)) for t in (cos64, s1_64, s2_64))
    nseq = seq // tm
    tab_spec = pl.BlockSpec((tm, LANES), lambda i: (i % nseq, 0))
    full = lambda shape: pl.BlockSpec(shape, lambda i: (0,) * len(shape))
    kern = functools.partial(_even_in_kernel, qk_w=qk_w, q_w=q_w, kv_w=kv_w, b_w=b_w)
    return pl.pallas_call(
        kern,
        grid=(n // tm,),
        in_specs=[pl.BlockSpec((tm, d), lambda i: (i, 0)), full((1, d)), full(w_in.shape),
                  full((1, qk_w)), full((qk_w, qk_w)), tab_spec, tab_spec, tab_spec],
        out_specs=[_chunkT_spec(q_w, tm, nseq), pl.BlockSpec((tm, kv_w), lambda i: (i, 0)),
                   _chunkT_spec(kv_w, tm, nseq), pl.BlockSpec((tm, b_w), lambda i: (i, 0))],
        out_shape=[jax.ShapeDtypeStruct((n // seq, nseq, q_w, tm), BF16), jax.ShapeDtypeStruct((n, kv_w), BF16),
                   jax.ShapeDtypeStruct((n // seq, nseq, kv_w, tm), BF16), jax.ShapeDtypeStruct((n, b_w), F32)],
        compiler_params=_cparams("parallel"),
        name="even_in_proj",
    )(x2d, g[None, :].astype(F32), w_in.astype(BF16), gqk, bd, cos_t, s1_t, s2_t)


def _flash_pipelined(nk, n_lead, n_maps, score_fn, shift_fn, value_fn, s_a, s_b, acc_ref, tq, dv):
    assert nk % 2 == 0 and n_lead % 2 == 0 and nk - n_lead >= 2
    ones = jnp.ones((BF16_ROWS, s_a.shape[1]), BF16)

    def scores(t, s_ref, lead):
        shift = shift_fn(t, lead)
        cms = []
        for c in range(n_maps):
            s = score_fn(t, c, lead)
            s_ref[c] = s
            cm = jnp.max(s, axis=0, keepdims=True)
            cms.append(cm if shift is None else cm + shift)
        return tuple(cms)

    def softmax(t, s_ref, cms, state, lead):
        shift = shift_fn(t, lead)
        vals = jnp.concatenate([value_fn(t), ones], axis=0)
        new = []
        for c in range(n_maps):
            m = state[c]
            m_new = jnp.maximum(m, cms[c])
            alpha = jnp.exp2(m - m_new)
            p = jnp.exp2(s_ref[c] - (m_new if shift is None else m_new - shift))
            acc_ref[c] = alpha * acc_ref[c] + jnp.dot(vals, p.astype(BF16), preferred_element_type=F32)
            new.append(m_new)
        return tuple(new)

    acc_ref[...] = jnp.zeros(acc_ref.shape, F32)
    state = tuple(jnp.full((1, tq), NEG_BIG, F32) for _ in range(n_maps))
    bufs = (s_a, s_b)
    cms = scores(0, s_a, n_lead > 0)
    for t in range(n_lead):
        nxt = scores(t + 1, bufs[(t + 1) % 2], t + 1 < n_lead)
        state = softmax(t, bufs[t % 2], cms, state, True)
        cms = nxt

    def pair(u, carry):
        cms_a, state = carry
        t = n_lead + 2 * u
        cms_b = scores(t + 1, s_b, False)
        state = softmax(t, s_a, cms_a, state, False)
        cms_a = scores(t + 2, s_a, False)
        state = softmax(t + 1, s_b, cms_b, state, False)
        return cms_a, state

    cms_a, state = lax.fori_loop(0, (nk - n_lead) // 2 - 1, pair, (cms, state))
    cms_b = scores(nk - 1, s_b, False)
    state = softmax(nk - 2, s_a, cms_a, state, False)
    state = softmax(nk - 1, s_b, cms_b, state, False)
    return tuple(acc_ref[c, :dv, :] / acc_ref[c, dv:dv + 1, :] for c in range(n_maps))


def _gqa_kernel(qT_ref, k_ref, vT_ref, o_ref, s_a, s_b, acc_ref, *, nk, tk, nh):
    hd = HEAD_DIM
    tq = qT_ref.shape[3]
    kv = pl.program_id(1) // (A_GROUP // nh)
    row_head = lax.broadcasted_iota(jnp.int32, (A_KV_HEADS * hd, tq), 0) // hd
    qs = []
    for h in range(nh):
        qh = qT_ref[0, 0, h * hd:(h + 1) * hd, :]
        qs.append(jnp.where(row_head == kv, jnp.concatenate([qh] * A_KV_HEADS, axis=0), jnp.zeros((), BF16)))

    def score_fn(t, c, lead):
        kc = k_ref[0, pl.ds(pl.multiple_of(t * tk, tk), tk), :]
        return jnp.dot(kc, qs[c], preferred_element_type=F32)

    res = _flash_pipelined(nk, 0, nh, score_fn, lambda t, lead: None, lambda t: vT_ref[0, t], s_a, s_b, acc_ref,
                           tq, hd)
    for h, o in enumerate(res):
        o_ref[0, 0, h * hd:(h + 1) * hd, :] = o.astype(o_ref.dtype)


def _gqa_attention(qT, k, vT, batch, seq):
    hd, nh = HEAD_DIM, HEADS_PER_STEP_A
    tq = qT.shape[3]
    tk = vT.shape[3]
    nk = seq // tk
    hsteps = A_HEADS // nh
    per_kv = A_GROUP // nh
    k3 = k.reshape(batch, seq, A_KV_HEADS * hd)
    return pl.pallas_call(
        functools.partial(_gqa_kernel, nk=nk, tk=tk, nh=nh),
        grid=(batch, hsteps, seq // tq),
        in_specs=[pl.BlockSpec((1, 1, nh * hd, tq), lambda b, h, i: (b, i, h, 0)),
                  pl.BlockSpec((1, seq, A_KV_HEADS * hd), lambda b, h, i: (b, 0, 0)),
                  pl.BlockSpec((1, nk, hd, tk), lambda b, h, i: (b, 0, h // per_kv, 0))],
        out_specs=pl.BlockSpec((1, 1, nh * hd, tq), lambda b, h, i: (b, i, h, 0)),
        out_shape=jax.ShapeDtypeStruct(qT.shape, BF16),
        scratch_shapes=[pltpu.VMEM((nh, tk, tq), F32), pltpu.VMEM((nh, tk, tq), F32),
                        pltpu.VMEM((nh, hd + BF16_ROWS, tq), F32)],
        compiler_params=_cparams("parallel", "parallel", "parallel"),
        name="gqa_attention",
    )(qT, k3, vT)


def _conv_kernel(prev_ref, cur_ref, next_ref, w_ref, b_ref, g_ref, beta_ref, bd_ref, o_ref, xe_ref, xs_ref, *, ts):
    i = pl.program_id(1)
    last = pl.num_programs(1) - 1
    halo = CONV_HALO
    xe_ref[0:halo, :] = jnp.where(i > 0, prev_ref[0], 0.0)
    xe_ref[halo:halo + ts, :] = cur_ref[0]
    xe_ref[halo + ts:halo + ts + halo, :] = jnp.where(i < last, next_ref[0], 0.0)
    rows = xs_ref.shape[1]
    for rho in range(1, SUBLANES):
        xs_ref[rho - 1] = xe_ref[rho:rho + rows, :]
    w = w_ref[...]
    first = halo - CONV_WIDTH // 2
    for r0 in range(0, ts, CONV_SUB):
        acc = jnp.zeros((CONV_SUB, w.shape[1]), F32) + b_ref[...]
        for j in range(CONV_WIDTH):
            rho = (first + j) % SUBLANES
            base = r0 + first + j - rho
            src = xe_ref[base:base + CONV_SUB, :] if rho == 0 else xs_ref[rho - 1, base:base + CONV_SUB, :]
            acc = acc + w[j:j + 1, :] * src
        mu = _split_dot(acc, bd_ref[...])
        cen = acc - mu
        var = _split_dot(cen * cen, bd_ref[...])
        hn = cen * lax.rsqrt(var + 1e-5) * g_ref[...] + beta_ref[...]
        o_ref[0, r0:r0 + CONV_SUB, :] = (hn * _sigmoid(hn)).astype(o_ref.dtype)


def _conformer_conv(glu, conv_w, conv_b, gn_g, gn_b, batch, seq):
    c = glu.shape[1]
    ts, halo = TS_CONV, CONV_HALO
    x3 = glu.reshape(batch, seq, c)
    hpb = ts // halo
    nh = seq // halo
    gid = jnp.arange(c) // B_GROUP_WIDTH
    bd = (gid[:, None] == gid[None, :]).astype(BF16) * (1.0 / B_GROUP_WIDTH)
    wpad = jnp.zeros((32, c), F32).at[:CONV_WIDTH].set(conv_w.astype(F32))
    vec = lambda a: a[None, :].astype(F32)
    full = lambda shape: pl.BlockSpec(shape, lambda b, i: (0,) * len(shape))
    out = pl.pallas_call(
        functools.partial(_conv_kernel, ts=ts),
        grid=(batch, seq // ts),
        in_specs=[pl.BlockSpec((1, halo, c), lambda b, i: (b, jnp.maximum(i * hpb - 1, 0), 0)),
                  pl.BlockSpec((1, ts, c), lambda b, i: (b, i, 0)),
                  pl.BlockSpec((1, halo, c), lambda b, i: (b, jnp.minimum((i + 1) * hpb, nh - 1), 0)),
                  full((32, c)), full((1, c)), full((1, c)), full((1, c)), full((c, c))],
        out_specs=pl.BlockSpec((1, ts, c), lambda b, i: (b, i, 0)),
        out_shape=jax.ShapeDtypeStruct((batch, seq, c), BF16),
        scratch_shapes=[pltpu.VMEM((ts + 2 * halo, c), F32),
                        pltpu.VMEM((SUBLANES - 1, ts + 2 * halo - SUBLANES, c), F32)],
        compiler_params=_cparams("parallel", "parallel"),
        name="conformer_conv",
    )(x3, x3, x3, wpad, vec(conv_b), vec(gn_g), vec(gn_b), bd)
    return out.reshape(batch * seq, c)


def _route(logits):
    lane = lax.broadcasted_iota(jnp.int32, logits.shape, 1)
    big = jnp.int32(1 << 20)
    ninf = jnp.float32(-jnp.inf)
    gmask = (lane >= MOE_EXPERTS) & (lane < MOE_EXPERTS + MOE_GROUPS)
    lg = jnp.where(gmask, logits, ninf)
    mg = jnp.max(lg, axis=-1, keepdims=True)
    sg = jnp.sum(jnp.exp(lg - mg), axis=-1, keepdims=True)
    g_val = 1.0 / sg
    g_idx = jnp.min(jnp.where(lg == mg, lane - MOE_EXPERTS, big), axis=-1, keepdims=True)
    emask = (lane < MOE_EXPERTS) & ((lane // MOE_PER_GROUP) == g_idx)
    le = jnp.where(emask, logits, ninf)
    m1 = jnp.max(le, axis=-1, keepdims=True)
    se = jnp.sum(jnp.exp(le - m1), axis=-1, keepdims=True)
    i1 = jnp.min(jnp.where(le == m1, lane, big), axis=-1, keepdims=True)
    le2 = jnp.where(lane == i1, ninf, le)
    m2 = jnp.max(le2, axis=-1, keepdims=True)
    i2 = jnp.min(jnp.where(le2 == m2, lane, big), axis=-1, keepdims=True)
    e1 = 1.0 / se
    e2 = jnp.exp(m2 - m1) / se
    norm = g_val / (e1 + e2)
    ids = jnp.where(lane == 0, i1, i2).astype(F32)
    gates = jnp.where(lane == MOE_TOPK, e1 * norm, e2 * norm)
    return jnp.where(lane < MOE_TOPK, ids, gates)


def _out_router_kernel(*refs, has_yb):
    if has_yb:
        yT_ref, yb_ref, x_ref, wa_ref, wb_ref = refs[:5]
        rest = refs[5:]
    else:
        yT_ref, x_ref, wa_ref = refs[:3]
        rest = refs[3:]
    g_ref, wrh_ref, wrl_ref, br_ref, x1_ref, h_ref, route_ref = rest
    ya = yT_ref[0, 0].astype(F32).T.astype(BF16)
    x1 = x_ref[...] + jnp.dot(ya, wa_ref[...], preferred_element_type=F32)
    if has_yb:
        x1 = x1 + jnp.dot(yb_ref[...], wb_ref[...], preferred_element_type=F32)
    x1_ref[...] = x1
    h = _rms_rows(x1, g_ref[...], EPS)
    _rows_to_tiles(h_ref, h)
    hi = h.astype(BF16)
    lo = (h - hi.astype(F32)).astype(BF16)
    logits = (jnp.dot(hi, wrh_ref[...], preferred_element_type=F32)
              + jnp.dot(hi, wrl_ref[...], preferred_element_type=F32)
              + jnp.dot(lo, wrh_ref[...], preferred_element_type=F32) + br_ref[...])
    route_ref[...] = _route(logits).T[:SUBLANES]


def _out_router(yT, yb, x2d, w_out, g_ffn, w_group, b_group, w_expert, b_expert):
    n, d = x2d.shape
    tm = yT.shape[3]
    nseq = yT.shape[1]
    wa = yT.shape[2]
    wb16 = w_out.astype(BF16)
    wr = jnp.zeros((d, LANES), F32).at[:, :MOE_EXPERTS].set(w_expert.astype(F32))
    wr = wr.at[:, MOE_EXPERTS:MOE_EXPERTS + MOE_GROUPS].set(w_group.astype(F32))
    wrh = wr.astype(BF16)
    wrl = (wr - wrh.astype(F32)).astype(BF16)
    br = jnp.zeros((1, LANES), F32).at[0, :MOE_EXPERTS].set(b_expert.astype(F32))
    br = br.at[0, MOE_EXPERTS:MOE_EXPERTS + MOE_GROUPS].set(b_group.astype(F32))
    full = lambda shape: pl.BlockSpec(shape, lambda i: (0,) * len(shape))
    row = lambda w: pl.BlockSpec((tm, w), lambda i: (i, 0))
    has_yb = yb is not None
    if has_yb:
        args = [yT, yb, x2d, wb16[:wa], wb16[wa:]]
        specs = [_chunkT_spec(wa, tm, nseq), row(yb.shape[1]), row(d), full((wa, d)), full((w_out.shape[0] - wa, d))]
    else:
        assert wa == w_out.shape[0]
        args = [yT, x2d, wb16]
        specs = [_chunkT_spec(wa, tm, nseq), row(d), full((wa, d))]
    return pl.pallas_call(
        functools.partial(_out_router_kernel, has_yb=has_yb),
        grid=(n // tm,),
        in_specs=specs + [full((1, d)), full((d, LANES)), full((d, LANES)), full((1, LANES))],
        out_specs=[row(d), pl.BlockSpec((tm * SUBLANES, LANES), lambda i: (i, 0)),
                   pl.BlockSpec((SUBLANES, tm), lambda i: (0, i))],
        out_shape=[jax.ShapeDtypeStruct((n, d), F32), jax.ShapeDtypeStruct((n * SUBLANES, LANES), F32),
                   jax.ShapeDtypeStruct((SUBLANES, n), F32)],
        compiler_params=_cparams("parallel"),
        name="out_proj_router",
    )(*args, g_ffn[None, :].astype(F32), wrh, wrl, br)


def _moe_dispatch(route, n):
    ids = route[:MOE_TOPK].astype(jnp.int32)
    gates = route[MOE_TOPK:2 * MOE_TOPK]
    bm = MOE_BM
    a = n * MOE_TOPK
    p = a + MOE_EXPERTS * bm
    flat_e = ids.reshape(-1)
    flat_w = gates.reshape(-1)
    order = jnp.argsort(flat_e).astype(jnp.int32)
    experts = jnp.arange(MOE_EXPERTS, dtype=jnp.int32)
    counts = jnp.sum((flat_e[:, None] == experts[None, :]).astype(jnp.int32), axis=0)
    pcounts = (counts + bm - 1) // bm * bm
    starts = jnp.cumsum(counts) - counts
    pends = jnp.cumsum(pcounts)
    pstarts = pends - pcounts
    n_blk = p // bm
    blk_start = jnp.arange(n_blk, dtype=jnp.int32) * bm
    blk_e = jnp.minimum(jnp.sum((blk_start[:, None] >= pends[None, :]).astype(jnp.int32), axis=1),
                        MOE_EXPERTS - 1)
    off = (blk_start - pstarts[blk_e])[:, None] + jnp.arange(bm, dtype=jnp.int32)[None, :]
    valid = (off < counts[blk_e][:, None]).reshape(p)
    sidx = jnp.clip(starts[blk_e][:, None] + off, 0, a - 1).reshape(p)
    assign = order[sidx]
    tok = assign % n
    slot = assign // n
    src_tok = jnp.where(valid, tok, 0)
    dump = a + jnp.cumsum(1 - valid.astype(jnp.int32)) - 1
    dst_row = jnp.where(valid, slot * n + tok, dump).astype(jnp.int32)
    row_w = jnp.where(valid, flat_w[assign], 0.0)
    tok_bits = (n - 1).bit_length()
    src_tok = jnp.concatenate([jnp.zeros((bm,), jnp.int32), src_tok])
    dst_row = jnp.concatenate([p + jnp.arange(bm, dtype=jnp.int32), dst_row])
    assert tok_bits + (p + 2 * bm - 1).bit_length() <= 32
    packed = lax.bitcast_convert_type(
        src_tok.astype(jnp.uint32) | (dst_row.astype(jnp.uint32) << tok_bits), jnp.int32)
    return blk_e, packed, row_w[:, None], tok_bits


def _moe_kernel(be_ref, idx_ref, h_hbm, w1_ref, w3_ref, w2_ref, rw_ref, y_hbm,
                xbuf, obuf, gsem, ssem, *, tok_bits, spare_row):
    del be_ref
    bm = MOE_BM
    i = pl.program_id(0)
    nb = pl.num_programs(0)
    slot = i % 2
    tok_mask = (1 << tok_bits) - 1

    tile = lambda r: pl.ds(pl.multiple_of(r * SUBLANES, SUBLANES), SUBLANES)

    def gather_row(blk, sl, r):
        t = idx_ref[(blk + 1) * bm + r] & tok_mask
        pltpu.make_async_copy(h_hbm.at[tile(t)], xbuf.at[sl, tile(r)], gsem.at[sl]).start()

    def scatter_row(blk, sl, r):
        t = lax.shift_right_logical(idx_ref[(blk + 1) * bm + r], tok_bits)
        pltpu.make_async_copy(obuf.at[sl, tile(r)], y_hbm.at[tile(t)], ssem.at[sl]).start()

    def row_loop(body):
        lax.fori_loop(0, bm, lambda r, c: (body(r), c)[1], 0, unroll=MOE_DMA_UNROLL)

    def gather_wait(sl):
        pltpu.make_async_copy(h_hbm.at[pl.ds(0, bm * SUBLANES)], xbuf.at[sl], gsem.at[sl]).wait()

    def scatter_wait(sl):
        pltpu.make_async_copy(obuf.at[sl], y_hbm.at[pl.ds(0, bm * SUBLANES)], ssem.at[sl]).wait()

    @pl.when(i == 0)
    def _():
        obuf[...] = jnp.zeros(obuf.shape, obuf.dtype)
        row_loop(lambda r: gather_row(0, 0, r))
        row_loop(lambda r: pltpu.make_async_copy(obuf.at[0, tile(r)], y_hbm.at[tile(spare_row + r)],
                                                 ssem.at[0]).start())

    gather_wait(slot)
    nxt = jnp.minimum(i + 1, nb - 1)
    for r in range(bm):
        gather_row(nxt, 1 - slot, r)
        scatter_row(i - 1, 1 - slot, r)
    x = _rows_from_tiles(xbuf.at[slot]).astype(BF16)
    h1 = jnp.dot(x, w1_ref[0, 0].astype(BF16), preferred_element_type=F32)
    h3 = jnp.dot(x, w3_ref[0, 0].astype(BF16), preferred_element_type=F32)
    hm = (h1 * _sigmoid(h1) * h3).astype(BF16)
    y = jnp.dot(hm, w2_ref[0, 0].astype(BF16), preferred_element_type=F32) * rw_ref[...]
    scatter_wait(slot)
    _rows_to_tiles(obuf.at[slot], y)

    @pl.when(i == nb - 1)
    def _():
        row_loop(lambda r: scatter_row(i, slot, r))
        scatter_wait(slot)
        scatter_wait(1 - slot)
        gather_wait(1 - slot)


def _moe_ffn(h_tiles, route, w1, w3, w2, layer):
    d = SUBLANES * LANES
    n = h_tiles.shape[0] // SUBLANES
    assert w1.shape[2] == d and h_tiles.shape[1] == LANES
    bm = MOE_BM
    hid = w1.shape[-1]
    blk_e, packed, row_w, tok_bits = _moe_dispatch(route, n)
    n_blk = blk_e.shape[0]
    rows_real = n * MOE_TOPK + MOE_EXPERTS * bm
    rows_out = rows_real + 2 * bm
    grid_spec = pltpu.PrefetchScalarGridSpec(
        num_scalar_prefetch=2,
        grid=(n_blk,),
        in_specs=[pl.BlockSpec(memory_space=pl.ANY),
                  pl.BlockSpec((1, 1, d, hid), lambda i, be, ix: (layer, be[i], 0, 0)),
                  pl.BlockSpec((1, 1, d, hid), lambda i, be, ix: (layer, be[i], 0, 0)),
                  pl.BlockSpec((1, 1, hid, d), lambda i, be, ix: (layer, be[i], 0, 0)),
                  pl.BlockSpec((bm, 1), lambda i, be, ix: (i, 0))],
        out_specs=pl.BlockSpec(memory_space=pl.ANY),
        scratch_shapes=[pltpu.VMEM((2, bm * SUBLANES, LANES), F32), pltpu.VMEM((2, bm * SUBLANES, LANES), F32),
                        pltpu.SemaphoreType.DMA((2,)), pltpu.SemaphoreType.DMA((2,))],
    )
    return pl.pallas_call(
        functools.partial(_moe_kernel, tok_bits=tok_bits, spare_row=rows_real + bm),
        grid_spec=grid_spec,
        out_shape=jax.ShapeDtypeStruct((rows_out * SUBLANES, LANES), F32),
        compiler_params=_cparams("arbitrary"),
        name="moe_ffn",
    )(blk_e, packed, h_tiles, w1, w3, w2, row_w)


def _odd_in_kernel(x_ref, y0_ref, y1_ref, g_ref, w_ref, xo_ref, q_ref, k_ref, v_ref, *, cw):
    x = x_ref[...] + _rows_from_tiles(y0_ref) + _rows_from_tiles(y1_ref)
    xo_ref[...] = x
    h = _rms_rows(x, g_ref[...], EPS)
    z = jnp.dot(h.astype(BF16), w_ref[...], preferred_element_type=F32)
    q_ref[0, 0] = (z[:, :cw] * (C_HEAD_DIM ** -0.5 * LOG2E)).T.astype(BF16)
    k_ref[...] = z[:, cw:2 * cw].astype(BF16)
    v_ref[0, 0] = z[:, 2 * cw:3 * cw].T.astype(BF16)


def _odd_in(x2d, ybuf, g, w_in, seq):
    n, d = x2d.shape
    tm = TM_PROJ
    cw = w_in.shape[1] // 3
    nt = n // tm
    row = lambda w: pl.BlockSpec((tm, w), lambda i: (i, 0))
    full = lambda shape: pl.BlockSpec(shape, lambda i: (0,) * len(shape))
    return pl.pallas_call(
        functools.partial(_odd_in_kernel, cw=cw),
        grid=(nt,),
        in_specs=[row(d), pl.BlockSpec((tm * SUBLANES, LANES), lambda i: (i, 0)),
                  pl.BlockSpec((tm * SUBLANES, LANES), lambda i: (i + nt, 0)), full((1, d)), full(w_in.shape)],
        out_specs=[row(d), _chunkT_spec(cw, tm, seq // tm), row(cw), _chunkT_spec(cw, tm, seq // tm)],
        out_shape=[jax.ShapeDtypeStruct((n, d), F32), jax.ShapeDtypeStruct((n // seq, seq // tm, cw, tm), BF16),
                   jax.ShapeDtypeStruct((n, cw), BF16), jax.ShapeDtypeStruct((n // seq, seq // tm, cw, tm), BF16)],
        compiler_params=_cparams("parallel"),
        name="odd_in_proj",
    )(x2d, ybuf, ybuf, g[None, :].astype(F32), w_in.astype(BF16))


def _t5_bucket(rel):
    nb = REL_BUCKETS // 2
    max_exact = nb // 2
    ret = jnp.where(rel > 0, nb, 0).astype(jnp.int32)
    n = jnp.abs(rel)
    nf = jnp.maximum(n, 1).astype(F32)
    large = max_exact + (jnp.log(nf / max_exact) / math.log(REL_MAX_DIST / max_exact) * (nb - max_exact)).astype(jnp.int32)
    large = jnp.minimum(large, nb - 1)
    return ret + jnp.where(n < max_exact, n, large)


def _diff_kernel(cfar_ref, lam_ref, qT_ref, k_ref, vT_ref, band_ref, gsub_ref, o_ref, s_a, s_b, acc_ref, *, nk, tk):
    head = pl.program_id(1)
    i = pl.program_id(2)
    hd = C_HEAD_DIM
    tq = qT_ref.shape[3]
    dv = vT_ref.shape[2]
    qT = qT_ref[0, 0]
    row_map = lax.broadcasted_iota(jnp.int32, qT.shape, 0) // hd
    qs = [jnp.where(row_map == c, qT, jnp.zeros((), BF16)) for c in range(2)]

    chunk = lambda t: (i + (nk - 1) + t) % nk

    def score_fn(t, c, lead):
        j = chunk(t)
        s = jnp.dot(k_ref[0, pl.ds(pl.multiple_of(j * tk, tk), tk), :], qs[c], preferred_element_type=F32)
        if lead:
            s = s + band_ref[0, jnp.clip(j - i, -2, 2) + 2]
        return s

    def shift_fn(t, lead):
        if lead:
            return None
        return jnp.where(chunk(t) < i, cfar_ref[head, 0], cfar_ref[head, 1])

    o0, o1 = _flash_pipelined(nk, DIFF_LEAD, 2, score_fn, shift_fn, lambda t: vT_ref[0, chunk(t)],
                              s_a, s_b, acc_ref, tq, dv)
    o = o0 - lam_ref[...] * o1
    ms = jnp.mean(o * o, axis=0, keepdims=True)
    o_ref[0, 0] = (o * lax.rsqrt(ms + 1e-5) * gsub_ref[...]).astype(o_ref.dtype)


def _bias_band(rel_bias, t):
    w = REL_MAX_DIST
    assert t % w == 0
    nb = t // w
    rb = rel_bias.astype(F32) * LOG2E
    nh = rb.shape[1]
    period = 2 * w
    m = jnp.arange(period, dtype=jnp.int32)
    e3 = jnp.arange(-1, 2, dtype=jnp.int32)[:, None] * w
    rel = jnp.where(m[None, :] < w, e3 - m[None, :], e3 + (period - m[None, :]))
    vec = rb[_t5_bucket(rel)].transpose(2, 0, 1)
    skew = jnp.tile(vec, (1, 1, w))[:, :, :w * (period - 1)].reshape(nh, 3, w, period - 1)[..., :w]
    const = lambda r: jnp.broadcast_to(rb[_t5_bucket(jnp.int32(r))][:, None, None], (nh, w, w))
    lo, hi = const(-2 * w), const(2 * w)
    block = lambda e: lo if e <= -2 else hi if e >= 2 else skew[:, e + 1]
    tiles = [jnp.concatenate([jnp.concatenate([block(d * nb + a - b) for b in range(nb)], axis=2)
                              for a in range(nb)], axis=1) for d in range(-2, 3)]
    return jnp.stack(tiles, axis=1)


def _diff_attention(qT, k, vT, lam, lam_init, subln, rel_bias, batch, seq):
    hd = C_HEAD_DIM
    dv = 2 * hd
    tq = qT.shape[3]
    tk = vT.shape[3]
    nk = seq // tk
    assert tk == tq
    k3 = k.reshape(batch, seq, C_HEADS * dv)
    assert nk >= DIFF_LEAD + 2
    band = _bias_band(rel_bias, tk)
    rb = rel_bias.astype(F32) * LOG2E
    cfar = jnp.stack([rb[_t5_bucket(jnp.int32(-2 * tk))], rb[_t5_bucket(jnp.int32(2 * tk))]], axis=1)
    lam_vec = jnp.full((1, tq), lam, F32)
    gsub = (subln.astype(F32) * (1.0 - lam_init))[:, None]
    return pl.pallas_call(
        functools.partial(_diff_kernel, nk=nk, tk=tk),
        grid=(batch, C_HEADS, seq // tq),
        in_specs=[pl.BlockSpec(memory_space=pltpu.SMEM),
                  pl.BlockSpec((1, tq), lambda b, h, i: (0, 0)),
                  pl.BlockSpec((1, 1, dv, tq), lambda b, h, i: (b, i, h, 0)),
                  pl.BlockSpec((1, seq, dv), lambda b, h, i: (b, 0, h)),
                  pl.BlockSpec((1, nk, dv, tk), lambda b, h, i: (b, 0, h, 0)),
                  pl.BlockSpec((1, 5, tk, tq), lambda b, h, i: (h, 0, 0, 0)),
                  pl.BlockSpec((dv, 1), lambda b, h, i: (0, 0))],
        out_specs=pl.BlockSpec((1, 1, dv, tq), lambda b, h, i: (b, i, h, 0)),
        out_shape=jax.ShapeDtypeStruct(qT.shape, BF16),
        scratch_shapes=[pltpu.VMEM((2, tk, tq), F32), pltpu.VMEM((2, tk, tq), F32),
                        pltpu.VMEM((2, dv + BF16_ROWS, tq), F32)],
        compiler_params=_cparams("parallel", "parallel", "parallel"),
        name="diff_attention",
    )(cfar, lam_vec, qT, k3, vT, band, gsub)


def _final_kernel(x_ref, y0_ref, y1_ref, g_ref, o_ref):
    x = x_ref[...] + _rows_from_tiles(y0_ref) + _rows_from_tiles(y1_ref)
    o_ref[...] = _rms_rows(x, g_ref[...], EPS)


def _final(x2d, ybuf, g):
    n, d = x2d.shape
    tm = TM_PROJ
    nt = n // tm
    row = pl.BlockSpec((tm, d), lambda i: (i, 0))
    return pl.pallas_call(
        _final_kernel,
        grid=(nt,),
        in_specs=[row, pl.BlockSpec((tm * SUBLANES, LANES), lambda i: (i, 0)),
                  pl.BlockSpec((tm * SUBLANES, LANES), lambda i: (i + nt, 0)), pl.BlockSpec((1, d), lambda i: (0, 0))],
        out_specs=row,
        out_shape=jax.ShapeDtypeStruct((n, d), F32),
        compiler_params=_cparams("parallel"),
        name="final_norm",
    )(x2d, ybuf, ybuf, g[None, :].astype(F32))


def kernel(x, norm_mix, norm_ffn, norm_final, ev_w_in, ev_q_norm, ev_k_norm, ev_conv_w, ev_conv_b, ev_gn_g, ev_gn_b, ev_w_out, od_w_in, od_lam_q1, od_lam_k1, od_lam_q2, od_lam_k2, od_subln, od_w_out, rel_bias, moe_w_group, moe_b_group, moe_w_expert, moe_b_expert, moe_w1, moe_w3, moe_w2):
    batch, seq, d = x.shape
    n = batch * seq
    x2d = x.reshape(n, d).astype(F32)

    qT, k, vT, glu = _even_in(x2d, norm_mix[0], ev_w_in[0], ev_q_norm[0], ev_k_norm[0], seq)
    yaT = _gqa_attention(qT, k, vT, batch, seq)
    yb = _conformer_conv(glu, ev_conv_w[0], ev_conv_b[0], ev_gn_g[0], ev_gn_b[0], batch, seq)
    x1, h, route = _out_router(yaT, yb, x2d, ev_w_out[0], norm_ffn[0],
                                    moe_w_group[0], moe_b_group[0], moe_w_expert[0], moe_b_expert[0])
    ybuf = _moe_ffn(h, route, moe_w1, moe_w3, moe_w2, 0)

    x2, qT, k, vT = _odd_in(x1, ybuf, norm_mix[1], od_w_in[0], seq)
    layer_idx = 1
    lam_init = 0.8 - 0.6 * math.exp(-0.3 * layer_idx)
    lam = (jnp.exp(jnp.sum(od_lam_q1[0].astype(F32) * od_lam_k1[0].astype(F32)))
           - jnp.exp(jnp.sum(od_lam_q2[0].astype(F32) * od_lam_k2[0].astype(F32))) + lam_init)
    oT = _diff_attention(qT, k, vT, lam, lam_init, od_subln[0], rel_bias, batch, seq)
    x3, h, route = _out_router(oT, None, x2, od_w_out[0], norm_ffn[1],
                                    moe_w_group[1], moe_b_group[1], moe_w_expert[1], moe_b_expert[1])
    ybuf = _moe_ffn(h, route, moe_w1, moe_w3, moe_w2, 1)
    out = _final(x3, ybuf, norm_final)
    return out.reshape(batch, seq, d)
```
